```python
import jax, jax.numpy as jnp
from jax import lax
import numpy as np

D_MODEL = 1024
BATCH = 4
SEQ = 4096
DEPTH = 4
DEC_BATCH = 32
DEC_SEQ = 1
PAST_LEN = 8192
PAGE_SIZE = 128

HEAD_DIM = 64
CONV_CH = D_MODEL // 4
NSA_HEADS = (3 * D_MODEL // 8) // HEAD_DIM
MOBA_HEADS = (3 * D_MODEL // 8) // HEAD_DIM
NSA_KV = 2
MOBA_KV = 2
GQA_REP = NSA_HEADS // NSA_KV
MOBA_REP = MOBA_HEADS // MOBA_KV
MIX_W = CONV_CH + NSA_HEADS * HEAD_DIM + MOBA_HEADS * HEAD_DIM
CONV_WIDTH = 31
CMP_STRIDE = 16
CMP_LEN = 2 * CMP_STRIDE
CMP_HIDDEN = 2 * HEAD_DIM
SEL_BLOCK = 64
N_SEL = 16
WINDOW = 512
BAND = 128
MOBA_BLOCK = 256
MOBA_TOPK = 3
Q_CHUNK = 64
FFN_HIDDEN = (-((-8 * D_MODEL) // (3 * 256))) * 256
D_IN = 2 * CONV_CH + NSA_HEADS * HEAD_DIM + 6 * NSA_KV * HEAD_DIM + 3 * NSA_HEADS + MOBA_HEADS * HEAD_DIM + 2 * MOBA_KV * HEAD_DIM
ROPE_THETA = 10000.0
EPS = 1e-6
NEG = -1e30
BIG = 1e30
ATTN_SCALE = HEAD_DIM ** -0.5

kernel_name = 'hymba_conformer_nsa_moba_decode_step'


def rmsnorm(x, g):
    xf = x.astype(jnp.float32)
    y = xf * lax.rsqrt(jnp.mean(xf * xf, axis=-1, keepdims=True) + EPS)
    return (y * g.astype(jnp.float32)).astype(x.dtype)


def layernorm(x, g, b):
    xf = x.astype(jnp.float32)
    mu = jnp.mean(xf, axis=-1, keepdims=True)
    var = jnp.mean(jnp.square(xf - mu), axis=-1, keepdims=True)
    y = (xf - mu) * lax.rsqrt(var + EPS) * g.astype(jnp.float32) + b.astype(jnp.float32)
    return y.astype(x.dtype)


def rope(x, pos):
    half = HEAD_DIM // 2
    inv = ROPE_THETA ** (-jnp.arange(half, dtype=jnp.float32) / half)
    ang = pos.astype(jnp.float32)[:, None] * inv[None, :]
    cos = jnp.cos(ang)[None, :, None, :]
    sin = jnp.sin(ang)[None, :, None, :]
    xf = x.astype(jnp.float32)
    x1, x2 = xf[..., :half], xf[..., half:]
    return jnp.concatenate([x1 * cos - x2 * sin, x2 * cos + x1 * sin], axis=-1).astype(x.dtype)


def gather_pages(pool, page_table):
    g = pool[page_table]
    return g.reshape(g.shape[0], g.shape[1] * g.shape[2], *g.shape[3:])


def to_blocks(rows, blk):
    b, l, g, d = rows.shape
    n = -(-l // blk)
    r = jnp.pad(rows, ((0, 0), (0, n * blk - l), (0, 0), (0, 0)))
    return r.reshape(b, n, blk, g, d).transpose(0, 3, 1, 2, 4)


def query_blocks(fn, q, pos):
    b, t = q.shape[:2]
    if t <= Q_CHUNK or t % Q_CHUNK:
        return fn(q, pos)
    n = t // Q_CHUNK
    qc = q.reshape(b, n, Q_CHUNK, *q.shape[2:]).swapaxes(0, 1)
    pc = pos.reshape(n, Q_CHUNK)
    out = lax.map(lambda a: fn(a[0], a[1]), (qc, pc))
    return jax.tree_util.tree_map(lambda o: o.swapaxes(0, 1).reshape(b, t, *o.shape[3:]), out)


def project(h, w_in, pos):
    b, t, _ = h.shape
    sizes = [2 * CONV_CH, NSA_HEADS * HEAD_DIM, 6 * NSA_KV * HEAD_DIM, 3 * NSA_HEADS,
             MOBA_HEADS * HEAD_DIM, 2 * MOBA_KV * HEAD_DIM]
    cuts = [int(c) for c in np.cumsum(sizes)[:-1]]
    zu, zq, zkv, zg, zqm, zkvm = jnp.split(h @ w_in, cuts, axis=-1)
    u = zu[..., :CONV_CH] * jax.nn.sigmoid(zu[..., CONV_CH:])

    def rot(r):
        return jnp.stack([rope(r[:, :, 0], pos), r[:, :, 1]], axis=2)

    q = rope(zq.reshape(b, t, NSA_HEADS, HEAD_DIM), pos)
    kv = zkv.reshape(b, t, 3, 2, NSA_KV, HEAD_DIM)
    cmp_rows, sel_rows, win_rows = rot(kv[:, :, 0]), rot(kv[:, :, 1]), rot(kv[:, :, 2])
    gates = jax.nn.sigmoid(zg.astype(jnp.float32)).reshape(b, t, 3, NSA_HEADS)
    qm = rope(zqm.reshape(b, t, MOBA_HEADS, HEAD_DIM), pos)
    moba_rows = rot(zkvm.reshape(b, t, 2, MOBA_KV, HEAD_DIM))
    return u, q, cmp_rows, sel_rows, win_rows, gates, qm, moba_rows


def conv_module(u, past, w, bias, ln_g, ln_b):
    ext = jnp.concatenate([past.astype(u.dtype), u], axis=1)
    y = lax.conv_general_dilated(ext, w[:, None, :].astype(u.dtype), window_strides=(1,), padding='VALID',
                                 dimension_numbers=('NWC', 'WIO', 'NWC'), feature_group_count=CONV_CH)
    y = layernorm(y + bias, ln_g, ln_b)
    return jax.nn.silu(y), ext[:, -(CONV_WIDTH - 1):]


def compress(rows, pe, w1, w2):
    b, l, g, d = rows.shape
    n_ch = -(-l // CMP_STRIDE)
    r = jnp.pad(rows, ((0, 0), (0, n_ch * CMP_STRIDE - l), (0, 0), (0, 0))).reshape(b, n_ch, CMP_STRIDE, g, d)
    blk = jnp.concatenate([r[:, :-1], r[:, 1:]], axis=2) + pe[:, None, :]
    flat = blk.transpose(0, 1, 3, 2, 4).reshape(b, n_ch - 1, g, CMP_LEN * d)
    return jax.nn.gelu(flat @ w1) @ w2


def nsa_cmp_sel(q, pos, kc, vc, c_end, ks_blk, vs_blk):
    b, t = q.shape[:2]
    n_sb = ks_blk.shape[2]
    ratio = SEL_BLOCK // CMP_STRIDE
    qg = q.reshape(b, t, NSA_KV, GQA_REP, HEAD_DIM).astype(jnp.float32) * ATTN_SCALE
    cmask = (c_end[None, :] <= pos[:, None])[None, :, None, None, :]
    s = jnp.where(cmask, jnp.einsum('btgrd,bngd->btgrn', qg, kc.astype(jnp.float32)), NEG)
    p = jnp.where(cmask, jax.nn.softmax(s, axis=-1), 0.0)
    o_cmp = jnp.einsum('btgrn,bngd->btgrd', p, vc.astype(jnp.float32))
    imp = p.sum(axis=3)
    imp = jnp.pad(imp, ((0, 0), (0, 0), (0, 0), (0, ratio * (n_sb + 1) - imp.shape[-1])))
    imp = imp.reshape(b, t, NSA_KV, n_sb + 1, ratio)
    w_blk = jnp.array([1.0] + [2.0] * (ratio - 1), dtype=jnp.float32)
    p_slc = imp[..., :-1, :] @ w_blk + imp[..., 1:, 0]
    cur = pos // SEL_BLOCK
    j = jnp.arange(n_sb)[None, :]
    valid = j <= cur[:, None]
    forced = (j == 0) | (j == cur[:, None]) | (j == cur[:, None] - 1)
    score = jnp.where(forced[None, :, None, :], BIG, jnp.where(valid[None, :, None, :], p_slc, NEG))
    _, idx = lax.top_k(score, min(N_SEL, n_sb))
    bi = jnp.arange(b)[:, None, None, None]
    gi = jnp.arange(NSA_KV)[None, None, :, None]
    ks = ks_blk[bi, gi, idx].astype(jnp.float32)
    vs = vs_blk[bi, gi, idx].astype(jnp.float32)
    kpos = idx[..., None] * SEL_BLOCK + jnp.arange(SEL_BLOCK)
    smask = (kpos <= pos[None, :, None, None, None])[:, :, :, None]
    s2 = jnp.where(smask, jnp.einsum('btgrd,btgksd->btgrks', qg, ks), NEG)
    p2 = jax.nn.softmax(s2.reshape(*s2.shape[:4], -1), axis=-1).reshape(s2.shape)
    o_sel = jnp.einsum('btgrks,btgksd->btgrd', p2, vs)
    return (o_cmp.reshape(b, t, NSA_HEADS, HEAD_DIM), o_sel.reshape(b, t, NSA_HEADS, HEAD_DIM))


def nsa_global(q, pos, cmp_rows, sel_rows, pe_k, pe_v, wk1, wk2, wv1, wv2):
    kc = compress(cmp_rows[:, :, 0], pe_k, wk1, wk2)
    vc = compress(cmp_rows[:, :, 1], pe_v, wv1, wv2)
    c_end = jnp.arange(kc.shape[1]) * CMP_STRIDE + CMP_LEN - 1
    ks_blk = to_blocks(sel_rows[:, :, 0], SEL_BLOCK)
    vs_blk = to_blocks(sel_rows[:, :, 1], SEL_BLOCK)
    return query_blocks(lambda qc, pc: nsa_cmp_sel(qc, pc, kc, vc, c_end, ks_blk, vs_blk), q, pos)


def window_prompt(q, k, v):
    b, s_len = q.shape[:2]
    nb = s_len // BAND
    nw = WINDOW // BAND

    def band(r):
        rp = jnp.pad(r, ((0, 0), (WINDOW, 0), (0, 0), (0, 0))).reshape(b, nb + nw, BAND, NSA_KV, HEAD_DIM)
        return jnp.concatenate([rp[:, i:i + nb] for i in range(nw + 1)], axis=2)

    kb, vb = band(k), band(v)
    qg = q.reshape(b, nb, BAND, NSA_KV, GQA_REP, HEAD_DIM).astype(jnp.float32) * ATTN_SCALE
    s = jnp.einsum('bnqgrd,bnkgd->bnqgrk', qg, kb.astype(jnp.float32))
    qq = jnp.arange(BAND)[:, None]
    kk = jnp.arange((nw + 1) * BAND)[None, :]
    diff = WINDOW + qq - kk
    kpos = jnp.arange(nb)[:, None] * BAND - WINDOW + kk
    mask = ((diff >= 0) & (diff <= WINDOW))[None] & (kpos >= 0)[:, None, :]
    s = jnp.where(mask[None, :, :, None, None, :], s, NEG)
    o = jnp.einsum('bnqgrk,bnkgd->bnqgrd', jax.nn.softmax(s, axis=-1), vb.astype(jnp.float32))
    return o.reshape(b, s_len, NSA_HEADS, HEAD_DIM)


def window_sample(q, pos, buf, new_rows):
    b, t = q.shape[:2]
    wb = buf.shape[1]
    rows = jnp.concatenate([buf.astype(new_rows.dtype), new_rows], axis=1)
    kpos = jnp.concatenate([pos[0] - wb + jnp.arange(wb), pos])
    diff = pos[:, None] - kpos[None, :]
    mask = ((diff >= 0) & (diff <= WINDOW))[None, :, None, None, :]
    qg = q.reshape(b, t, NSA_KV, GQA_REP, HEAD_DIM).astype(jnp.float32) * ATTN_SCALE
    s = jnp.where(mask, jnp.einsum('btgrd,bkgd->btgrk', qg, rows[:, :, 0].astype(jnp.float32)), NEG)
    o = jnp.einsum('btgrk,bkgd->btgrd', jax.nn.softmax(s, axis=-1), rows[:, :, 1].astype(jnp.float32))
    return o.reshape(b, t, NSA_HEADS, HEAD_DIM)


def moba_attn(q, pos, kb, vb, kmean):
    b, t = q.shape[:2]
    nb = kb.shape[2]
    qg = q.reshape(b, t, MOBA_KV, MOBA_REP, HEAD_DIM).astype(jnp.float32)
    own = pos // MOBA_BLOCK
    gate = jnp.einsum('btgrd,bgnd->btgrn', qg, kmean)
    past = (jnp.arange(nb)[None, :] < own[:, None])[None, :, None, None, :]
    gate = jnp.where(past, gate, NEG)
    n_top = min(MOBA_TOPK, nb)
    _, idx = lax.top_k(gate, n_top)
    sel_ok = idx < own[None, :, None, None, None]
    bi = jnp.arange(b)[:, None, None, None, None]
    gi = jnp.arange(MOBA_KV)[None, None, :, None, None]
    ks = kb[bi, gi, idx].astype(jnp.float32)
    vs = vb[bi, gi, idx].astype(jnp.float32)
    bo = jnp.arange(b)[:, None, None]
    go = jnp.arange(MOBA_KV)[None, None, :]
    ko = kb[bo, go, own[None, :, None]].astype(jnp.float32)
    vo = vb[bo, go, own[None, :, None]].astype(jnp.float32)
    qs = qg * ATTN_SCALE
    s_sel = jnp.where(sel_ok[..., None], jnp.einsum('btgrd,btgrksd->btgrks', qs, ks), NEG)
    s_sel = s_sel.reshape(b, t, MOBA_KV, MOBA_REP, n_top * MOBA_BLOCK)
    own_pos = own[:, None] * MOBA_BLOCK + jnp.arange(MOBA_BLOCK)[None, :]
    own_ok = (own_pos <= pos[:, None])[None, :, None, None, :]
    s_own = jnp.where(own_ok, jnp.einsum('btgrd,btgsd->btgrs', qs, ko), NEG)
    p = jax.nn.softmax(jnp.concatenate([s_sel, s_own], axis=-1), axis=-1)
    p_sel = p[..., :n_top * MOBA_BLOCK].reshape(b, t, MOBA_KV, MOBA_REP, n_top, MOBA_BLOCK)
    o = jnp.einsum('btgrks,btgrksd->btgrd', p_sel, vs) + jnp.einsum('btgrs,btgsd->btgrd', p[..., n_top * MOBA_BLOCK:], vo)
    return o.reshape(b, t, MOBA_HEADS, HEAD_DIM)


def moba_global(qm, pos, moba_rows):
    kb = to_blocks(moba_rows[:, :, 0], MOBA_BLOCK)
    vb = to_blocks(moba_rows[:, :, 1], MOBA_BLOCK)
    kmean = kb.astype(jnp.float32).mean(axis=3)
    return query_blocks(lambda qc, pc: moba_attn(qc, pc, kb, vb, kmean), qm, pos)


def run_layer(x, pos, lp, past):
    b, t, _ = x.shape
    h = rmsnorm(x, lp['g_mix'])
    u, q, cmp_rows, sel_rows, win_rows, gates, qm, moba_rows = project(h, lp['w_in'], pos)
    if past is None:
        conv_past = jnp.zeros((b, CONV_WIDTH - 1, CONV_CH), u.dtype)
        cmp_all, sel_all, moba_all = cmp_rows, sel_rows, moba_rows
        o_win = window_prompt(q, win_rows[:, :, 0], win_rows[:, :, 1])
        win_state = win_rows[:, -min(WINDOW, t):]
    else:
        conv_past = past['conv']
        pt = past['pt']
        cmp_all = jnp.concatenate([gather_pages(past['cmp'], pt).astype(cmp_rows.dtype), cmp_rows], axis=1)
        sel_all = jnp.concatenate([gather_pages(past['sel'], pt).astype(sel_rows.dtype), sel_rows], axis=1)
        moba_all = jnp.concatenate([gather_pages(past['moba'], pt).astype(moba_rows.dtype), moba_rows], axis=1)
        buf = past['win']
        o_win = window_sample(q, pos, buf, win_rows)
        win_state = jnp.concatenate([buf.astype(win_rows.dtype), win_rows], axis=1)[:, -buf.shape[1]:]
    conv_o, conv_state = conv_module(u, conv_past, lp['conv_w'], lp['conv_b'], lp['conv_ln_g'], lp['conv_ln_b'])
    o_cmp, o_sel = nsa_global(q, pos, cmp_all, sel_all, lp['pe_k'], lp['pe_v'], lp['wk1'], lp['wk2'], lp['wv1'], lp['wv2'])
    nsa_o = gates[:, :, 0, :, None] * o_cmp + gates[:, :, 1, :, None] * o_sel + gates[:, :, 2, :, None] * o_win
    moba_o = moba_global(qm, pos, moba_all)
    mix = jnp.concatenate([conv_o.astype(x.dtype),
                           nsa_o.reshape(b, t, NSA_HEADS * HEAD_DIM).astype(x.dtype),
                           moba_o.reshape(b, t, MOBA_HEADS * HEAD_DIM).astype(x.dtype)], axis=-1)
    x = x + mix @ lp['w_out']
    h2 = rmsnorm(x, lp['g_ffn'])
    gt, up = jnp.split(h2 @ lp['w_gate_up'], 2, axis=-1)
    x = x + (jax.nn.silu(gt) * up) @ lp['w_down']
    return x, (cmp_rows, sel_rows, moba_rows, win_state, conv_state)


def setup_inputs(seed: int = 0) -> dict:
    key = jax.random.key(seed)
    ks = jax.random.split(key, 26)

    def nrm(k, shape, scale):
        return jax.random.normal(k, shape, jnp.float32) * scale

    n_pages = PAST_LEN // PAGE_SIZE
    n_used = DEC_BATCH * n_pages
    n_pool = n_used + (-(-n_used // 4))
    win_buf = min(WINDOW, PAST_LEN)
    page_table = jax.random.permutation(ks[7], n_pool)[:n_used].reshape(DEC_BATCH, n_pages).astype(jnp.int32)
    return {
        'x_prompt': nrm(ks[0], (BATCH, SEQ, D_MODEL), 1.0),
        'x_sample': nrm(ks[1], (DEC_BATCH, DEC_SEQ, D_MODEL), 1.0),
        'cache_nsa_cmp': nrm(ks[2], (DEPTH, n_pool, PAGE_SIZE, 2, NSA_KV, HEAD_DIM), 1.0),
        'cache_nsa_sel': nrm(ks[3], (DEPTH, n_pool, PAGE_SIZE, 2, NSA_KV, HEAD_DIM), 1.0),
        'cache_moba': nrm(ks[4], (DEPTH, n_pool, PAGE_SIZE, 2, MOBA_KV, HEAD_DIM), 1.0),
        'state_nsa_win': nrm(ks[5], (DEPTH, DEC_BATCH, win_buf, 2, NSA_KV, HEAD_DIM), 1.0),
        'state_conv': nrm(ks[6], (DEPTH, DEC_BATCH, CONV_WIDTH - 1, CONV_CH), 0.5),
        'page_table': page_table,
        'g_mix': 1.0 + nrm(ks[8], (DEPTH, D_MODEL), 0.02),
        'w_in': nrm(ks[9], (DEPTH, D_MODEL, D_IN), D_MODEL ** -0.5),
        'conv_w': nrm(ks[10], (DEPTH, CONV_WIDTH, CONV_CH), CONV_WIDTH ** -0.5),
        'conv_b': nrm(ks[11], (DEPTH, CONV_CH), 0.02),
        'conv_ln_g': 1.0 + nrm(ks[12], (DEPTH, CONV_CH), 0.02),
        'conv_ln_b': nrm(ks[13], (DEPTH, CONV_CH), 0.02),
        'cmp_pe_k': nrm(ks[14], (DEPTH, CMP_LEN, HEAD_DIM), 0.1),
        'cmp_pe_v': nrm(ks[15], (DEPTH, CMP_LEN, HEAD_DIM), 0.1),
        'cmp_wk1': nrm(ks[16], (DEPTH, CMP_LEN * HEAD_DIM, CMP_HIDDEN), (CMP_LEN * HEAD_DIM) ** -0.5),
        'cmp_wk2': nrm(ks[17], (DEPTH, CMP_HIDDEN, HEAD_DIM), CMP_HIDDEN ** -0.5),
        'cmp_wv1': nrm(ks[18], (DEPTH, CMP_LEN * HEAD_DIM, CMP_HIDDEN), (CMP_LEN * HEAD_DIM) ** -0.5),
        'cmp_wv2': nrm(ks[19], (DEPTH, CMP_HIDDEN, HEAD_DIM), CMP_HIDDEN ** -0.5),
        'w_out': nrm(ks[20], (DEPTH, MIX_W, D_MODEL), MIX_W ** -0.5),
        'g_ffn': 1.0 + nrm(ks[21], (DEPTH, D_MODEL), 0.02),
        'w_gate_up': nrm(ks[22], (DEPTH, D_MODEL, 2 * FFN_HIDDEN), D_MODEL ** -0.5),
        'w_down': nrm(ks[23], (DEPTH, FFN_HIDDEN, D_MODEL), FFN_HIDDEN ** -0.5),
        'g_final': 1.0 + nrm(ks[24], (D_MODEL,), 0.02),
    }


def reference(x_prompt, x_sample, cache_nsa_cmp, cache_nsa_sel, cache_moba, state_nsa_win, state_conv, page_table,
              g_mix, w_in, conv_w, conv_b, conv_ln_g, conv_ln_b, cmp_pe_k, cmp_pe_v, cmp_wk1, cmp_wk2, cmp_wv1, cmp_wv2,
              w_out, g_ffn, w_gate_up, w_down, g_final):
    pos_p = jnp.arange(x_prompt.shape[1])
    pos_s = PAST_LEN + jnp.arange(x_sample.shape[1])
    xp, xs = x_prompt, x_sample
    new_p = [[], [], [], [], []]
    new_s = [[], [], [], [], []]
    for l in range(DEPTH):
        lp = {'g_mix': g_mix[l], 'w_in': w_in[l], 'conv_w': conv_w[l], 'conv_b': conv_b[l],
              'conv_ln_g': conv_ln_g[l], 'conv_ln_b': conv_ln_b[l], 'pe_k': cmp_pe_k[l], 'pe_v': cmp_pe_v[l],
              'wk1': cmp_wk1[l], 'wk2': cmp_wk2[l], 'wv1': cmp_wv1[l], 'wv2': cmp_wv2[l],
              'w_out': w_out[l], 'g_ffn': g_ffn[l], 'w_gate_up': w_gate_up[l], 'w_down': w_down[l]}
        xp, st_p = run_layer(xp, pos_p, lp, None)
        past = {'cmp': cache_nsa_cmp[l], 'sel': cache_nsa_sel[l], 'moba': cache_moba[l],
                'win': state_nsa_win[l], 'conv': state_conv[l], 'pt': page_table}
        xs, st_s = run_layer(xs, pos_s, lp, past)
        for i in range(5):
            new_p[i].append(st_p[i])
            new_s[i].append(st_s[i])
    y_prompt = rmsnorm(xp, g_final)
    y_sample = rmsnorm(xs, g_final)
    return (y_prompt, y_sample,
            jnp.stack(new_p[0]), jnp.stack(new_s[0]),
            jnp.stack(new_p[1]), jnp.stack(new_s[1]),
            jnp.stack(new_p[2]), jnp.stack(new_s[2]),
            jnp.stack(new_p[3]), jnp.stack(new_s[3]),
            jnp.stack(new_p[4]), jnp.stack(new_s[4]))
```

```python
import functools

import numpy as np
import jax
import jax.numpy as jnp
from jax import lax
from jax.experimental import pallas as pl
from jax.experimental.pallas import tpu as pltpu

F32 = jnp.float32
BF16 = jnp.bfloat16
I32 = jnp.int32

D_MODEL = 1024
HEAD_DIM = 64
LANES = 128
CONV_CH = 256
N_HEADS = 6
N_KV = 2
REP = N_HEADS // N_KV
CONV_WIDTH = 31
CMP_STRIDE = 16
CMP_HIDDEN = 128
SEL_BLOCK = 64
N_SEL = 16
WINDOW = 512
BAND = 128
MOBA_BLOCK = 256
MOBA_TOPK = 3
PAGE = 128
FFN_HIDDEN = 2816
ROPE_THETA = 10000.0
EPS = 1e-6
NEG = -1e30
BIG = 1e30
ATTN_SCALE = HEAD_DIM ** -0.5
HI = lax.Precision.HIGHEST

KV_W = 2 * N_KV * HEAD_DIM
QPAD_W = N_HEADS * LANES
CHUNK_W = CMP_STRIDE * KV_W

OFF_U, OFF_Q, OFF_KV, OFF_QM, OFF_KVM, OFF_G, W_IN_COLS = 0, 512, 1280, 2048, 2816, 3072, 3200

VMEM_LIMIT = 56 * 1024 * 1024


def _params(n_axes, limit=VMEM_LIMIT):
    return pltpu.CompilerParams(dimension_semantics=("arbitrary",) * n_axes, vmem_limit_bytes=limit)


def _const_spec(shape):
    nd = len(shape)
    return pl.BlockSpec(shape, lambda *_: (0,) * nd, pipeline_mode=pl.Buffered(1))


def _dot(a, b, precision=None):
    return jnp.dot(a, b, preferred_element_type=F32, precision=precision)


def _dot_nt(a, b, precision=None):
    return lax.dot_general(a, b, (((1,), (1,)), ((), ())), preferred_element_type=F32, precision=precision)


def _iota(shape, dim):
    return lax.broadcasted_iota(I32, shape, dim)


def _rms(x, g):
    return x * lax.rsqrt(jnp.mean(x * x, axis=-1, keepdims=True) + EPS) * g


def _compact_heads(a):
    lo = _iota(a[0].shape, 1) < HEAD_DIM
    return jnp.concatenate([
        jnp.where(lo, a[0], pltpu.roll(a[1], HEAD_DIM, 1)),
        jnp.where(lo, a[2], a[3]),
        jnp.where(lo, pltpu.roll(a[4], HEAD_DIM, 1), a[5])], axis=1)


def _proj_kernel(x_ref, g_ref, w_ref, cos_ref, sa_ref, sb_ref,
                 u_ref, q_ref, cmp_ref, sel_ref, win_ref, moba_ref, qm_ref, gates_ref,
                 selk_ref, selv_ref, wink_ref, winv_ref, mobak_ref, mobav_ref, *, seq, tm):
    i = pl.program_id(0)
    h = _rms(x_ref[...], g_ref[...]).astype(BF16)
    cos, sa, sb = cos_ref[...], sa_ref[...], sb_ref[...]

    def seg(off, width):
        return _dot(h, w_ref[:, off:off + width])

    def rope(z):
        return z * cos + pltpu.roll(z, LANES - 32, 1) * sa + pltpu.roll(z, 32, 1) * sb

    zu = seg(OFF_U, 2 * CONV_CH)
    u_ref[...] = zu[:, :CONV_CH] * jax.nn.sigmoid(zu[:, CONV_CH:])

    for off, ref in ((OFF_Q, q_ref), (OFF_QM, qm_ref)):
        z = seg(off, QPAD_W)
        for hh in range(N_HEADS):
            ref[:, hh * LANES:(hh + 1) * LANES] = rope(z[:, hh * LANES:(hh + 1) * LANES]) * ATTN_SCALE

    t = (i * tm) % seq + _iota((tm, LANES), 0)
    lane = _iota((tm, LANES), 1)
    lo = lane < HEAD_DIM
    oh_sel = jnp.where((lane >= HEAD_DIM) & ((t >> 6) == lane - HEAD_DIM), NEG, 0.0)
    oh_moba = jnp.where((lane >= HEAD_DIM) & (lane < HEAD_DIM + REP * MOBA_LANES)
                        & ((t >> 8) == ((lane - HEAD_DIM) & (MOBA_LANES - 1))), NEG, 0.0)

    def rows(off, ref):
        z = seg(off, KV_W)
        k = rope(z[:, :LANES])
        v = z[:, LANES:]
        ref[:, :LANES] = k
        ref[:, LANES:] = v
        return k, v

    rows(OFF_KV, cmp_ref)
    k, v = rows(OFF_KV + KV_W, sel_ref)
    selk_ref[:, :LANES] = jnp.where(lo, k, oh_sel).astype(BF16)
    selk_ref[:, LANES:] = jnp.where(lo, pltpu.roll(k, HEAD_DIM, 1), oh_sel).astype(BF16)
    selv_ref[...] = v.astype(BF16)
    k, v = rows(OFF_KV + 2 * KV_W, win_ref)
    wink_ref[:, :LANES] = k.astype(BF16)
    wink_ref[:, LANES:] = pltpu.roll(k, HEAD_DIM, 1).astype(BF16)
    winv_ref[...] = v.astype(BF16)
    k, v = rows(OFF_KVM, moba_ref)
    mobak_ref[:, :LANES] = jnp.where(lo, k, oh_moba).astype(BF16)
    mobak_ref[:, LANES:] = jnp.where(lo, pltpu.roll(k, HEAD_DIM, 1), oh_moba).astype(BF16)
    mobav_ref[...] = v.astype(BF16)

    gates_ref[...] = jax.nn.sigmoid(seg(OFF_G, LANES))


def _proj(x, g, w, cos, sa, sb, seq, tm):
    n = x.shape[0]
    nt = cos.shape[0] // tm
    row = lambda wd: pl.BlockSpec((tm, wd), lambda i: (i, 0))
    tab = pl.BlockSpec((tm, LANES), lambda i: (i % nt, 0))
    f32_w = (CONV_CH, QPAD_W, KV_W, KV_W, KV_W, KV_W, QPAD_W, LANES)
    bf_w = (2 * LANES, LANES, 2 * LANES, LANES, 2 * LANES, LANES)
    return pl.pallas_call(
        functools.partial(_proj_kernel, seq=seq, tm=tm),
        grid=(n // tm,),
        in_specs=[row(D_MODEL), _const_spec((1, D_MODEL)), _const_spec((D_MODEL, W_IN_COLS)), tab, tab, tab],
        out_specs=[row(wd) for wd in f32_w + bf_w],
        out_shape=[jax.ShapeDtypeStruct((n, wd), F32) for wd in f32_w]
        + [jax.ShapeDtypeStruct((n, wd), BF16) for wd in bf_w],
        compiler_params=_params(1),
        name="proj",
    )(x, g, w, cos, sa, sb)


CONV_HALO = 32


def _conv_post(y, b_ref, lg_ref, lb_ref):
    y = y + b_ref[...]
    mu = jnp.mean(y, axis=-1, keepdims=True)
    var = jnp.mean(jnp.square(y - mu), axis=-1, keepdims=True)
    y = (y - mu) * lax.rsqrt(var + EPS) * lg_ref[...] + lb_ref[...]
    return y * jax.nn.sigmoid(y)


def _conv_kernel(u_ref, prev_ref, w_ref, b_ref, lg_ref, lb_ref, o_ref, ext_ref, *, tq):
    i = pl.program_id(1)
    ext_ref[:CONV_HALO, :] = jnp.where(i > 0, prev_ref[...], 0.0)
    ext_ref[CONV_HALO:, :] = u_ref[...]
    off = CONV_HALO - (CONV_WIDTH - 1)
    y = jnp.zeros((tq, CONV_CH), F32)
    for j in range(CONV_WIDTH):
        y = y + ext_ref[pl.ds(off + j, tq), :] * w_ref[j:j + 1, :]
    o_ref[...] = _conv_post(y, b_ref, lg_ref, lb_ref).astype(BF16)


def _conv(u, w, b, lg, lb, nb, seq, tq):
    n = u.shape[0]
    nq = seq // tq
    per = tq // CONV_HALO
    return pl.pallas_call(
        functools.partial(_conv_kernel, tq=tq),
        grid=(nb, nq),
        in_specs=[pl.BlockSpec((tq, CONV_CH), lambda b_, i: (b_ * nq + i, 0)),
                  pl.BlockSpec((CONV_HALO, CONV_CH), lambda b_, i: (jnp.maximum((b_ * nq + i) * per - 1, 0), 0)),
                  _const_spec((CONV_HALO, CONV_CH)), _const_spec((1, CONV_CH)), _const_spec((1, CONV_CH)),
                  _const_spec((1, CONV_CH))],
        out_specs=pl.BlockSpec((tq, CONV_CH), lambda b_, i: (b_ * nq + i, 0)),
        out_shape=jax.ShapeDtypeStruct((n, CONV_CH), BF16),
        scratch_shapes=[pltpu.VMEM((tq + CONV_HALO, CONV_CH), F32)],
        compiler_params=_params(2),
        name="conv",
    )(u, u, w, b, lg, lb)


def _gelu_tanh(x):
    return x * (0.5 * (1.0 + jnp.tanh(0.7978845608028654 * (x + 0.044715 * (x * x * x)))))


CMP_TAIL = 16


def _compress(x_bf, n_tok, wbig_ref, w2_ref, pe_ref, ab_ref):
    half = 4 * CMP_HIDDEN
    ab_ref[...] = _dot(x_bf, wbig_ref[...])
    pw = _dot(pe_ref[...].astype(BF16), wbig_ref[...])
    pe_all = pw[0:1, :half] + pw[1:2, half:]
    hid = ab_ref[pl.ds(0, n_tok), :half] + ab_ref[pl.ds(1, n_tok), half:] + pe_all
    return _dot(_gelu_tanh(hid).astype(BF16), w2_ref[...])


def _compress_kernel(x_ref, wbig_ref, w2_ref, pe_ref, o_ref, x_scr, ab_ref, *, n_tok):
    x_scr[pl.ds(0, n_tok), :] = x_ref[0].astype(BF16)
    x_scr[pl.ds(n_tok, CMP_TAIL), :] = jnp.zeros((CMP_TAIL, CHUNK_W), BF16)
    o_ref[0] = _compress(x_scr[...], n_tok, wbig_ref, w2_ref, pe_ref, ab_ref)


def _compress_prompt(rows, wbig, w2, pe, nb, seq):
    n_tok = seq // CMP_STRIDE
    x = rows.reshape(nb, n_tok, CHUNK_W)
    return pl.pallas_call(
        functools.partial(_compress_kernel, n_tok=n_tok),
        grid=(nb,),
        in_specs=[pl.BlockSpec((1, n_tok, CHUNK_W), lambda b_: (b_, 0, 0)),
                  _const_spec((CHUNK_W, 8 * CMP_HIDDEN)), _const_spec((4 * CMP_HIDDEN, KV_W)),
                  _const_spec((8, CHUNK_W))],
        out_specs=pl.BlockSpec((1, n_tok, KV_W), lambda b_: (b_, 0, 0)),
        out_shape=jax.ShapeDtypeStruct((nb, n_tok, KV_W), F32),
        scratch_shapes=[pltpu.VMEM((n_tok + CMP_TAIL, CHUNK_W), BF16),
                        pltpu.VMEM((n_tok + CMP_TAIL, 8 * CMP_HIDDEN), F32)],
        compiler_params=_params(1),
        name="compress_prompt",
    )(x, wbig, w2, pe)


def _softmax_rows(s, mask):
    s = jnp.where(mask, s, NEG)
    m = jnp.max(s, axis=-1, keepdims=True)
    e = jnp.where(mask, jnp.exp(s - m), 0.0)
    l = jnp.sum(e, axis=-1, keepdims=True)
    return e / jnp.where(l > 0.0, l, 1.0)


def _rank_rows(score, n):
    jj = _iota(score.shape, 0)
    rank = jnp.zeros(score.shape, I32)
    for ii in range(n):
        row = score[ii:ii + 1, :]
        rank = rank + jnp.where((row > score) | ((row == score) & (ii < jj)), 1, 0)
    return rank


def _flash_step(carry, s, v):
    m, l, acc = carry
    m_new = jnp.maximum(m, jnp.max(s, axis=-1, keepdims=True))
    a = jnp.exp(m - m_new)
    p = jnp.exp(s - m_new)
    return m_new, a * l + jnp.sum(p, axis=-1, keepdims=True), a * acc + _dot(p.astype(BF16), v)


def _flash_init(rows):
    return jnp.full((rows, 1), NEG, F32), jnp.zeros((rows, 1), F32), jnp.zeros((rows, LANES), F32)


SEL_TK = 256


def _nsa_kernel(q_ref, gates_ref, comp_ref, selk_ref, selv_ref, wink_ref, winv_ref, mwt_ref, o_ref, *, tq):
    i = pl.program_id(1)
    t0 = i * tq
    q = q_ref[...]
    comp = comp_ref[0]
    n_c = comp.shape[0]
    kc = comp[:, :LANES]
    kc_g = (kc, pltpu.roll(kc, HEAD_DIM, 1))
    vc = comp[:, LANES:].astype(BF16)
    pos_col = t0 + _iota((tq, 1), 0)
    cmask = (_iota((1, n_c), 1) * CMP_STRIDE + (2 * CMP_STRIDE - 1)) <= pos_col
    n_sb = mwt_ref.shape[0]

    def head(hh):
        return q[:, hh * LANES:(hh + 1) * LANES]

    o_cmp, ns = [], []
    for g in range(N_KV):
        imp = jnp.zeros((tq, n_c), F32)
        for r in range(REP):
            p = _softmax_rows(_dot_nt(head(g * REP + r), kc_g[g], HI), cmask)
            imp = imp + p
            o_cmp.append(_dot(p.astype(BF16), vc))
        pslc_t = _dot_nt(mwt_ref[...], imp, HI)
        jj = _iota((n_sb, tq), 0)
        cur = (t0 + _iota((n_sb, tq), 1)) >> 6
        forced = (jj == 0) | (jj == cur) | (jj == cur - 1)
        score = jnp.where(forced, BIG, jnp.where(jj <= cur, pslc_t, NEG))
        notsel = jnp.where(_rank_rows(score, n_sb) < N_SEL, 0.0, 1.0)
        parts = [jnp.zeros((HEAD_DIM, tq), F32), notsel]
        if n_sb < HEAD_DIM:
            parts.append(jnp.zeros((HEAD_DIM - n_sb, tq), F32))
        ns.append(jnp.concatenate(parts, axis=0).T.astype(BF16))

    row_pos = t0 + (_iota((REP * tq, 1), 0) % tq)

    o_sel = []
    for g in range(N_KV):
        qa = jnp.concatenate([head(g * REP + r).astype(BF16) + ns[g] for r in range(REP)], axis=0)

        def tile(kt, g=g, qa=qa):
            k0 = pl.multiple_of(kt * SEL_TK, SEL_TK)
            s = _dot_nt(qa, selk_ref[pl.ds(k0, SEL_TK), g * LANES:(g + 1) * LANES])
            return s, selv_ref[pl.ds(k0, SEL_TK), :]

        def body(kt, carry):
            s, v = tile(kt)
            return _flash_step(carry, s, v)

        last = t0 // SEL_TK
        carry = lax.fori_loop(0, last, body, _flash_init(REP * tq))
        s, v = tile(last)
        kpos = last * SEL_TK + _iota((1, SEL_TK), 1)
        m, l, acc = _flash_step(carry, jnp.where(kpos <= row_pos, s, NEG), v)
        o = acc / l
        o_sel += [o[r * tq:(r + 1) * tq] for r in range(REP)]

    o_win = []
    n_w = WINDOW // BAND + 1
    cc = _iota((1, BAND), 1)
    rr = _iota((REP * tq, 1), 0) % tq
    for g in range(N_KV):
        qw = jnp.concatenate([head(g * REP + r).astype(BF16) for r in range(REP)], axis=0)
        ss, vs = [], []
        for j in range(n_w):
            k0 = t0 - WINDOW + j * BAND
            ok = k0 >= 0
            k0c = pl.multiple_of(jnp.maximum(k0, 0), BAND)
            s = _dot_nt(qw, wink_ref[pl.ds(k0c, BAND), g * LANES:(g + 1) * LANES])
            if j == 0:
                ok = ok & (cc >= rr)
            elif j == n_w - 1:
                ok = ok & (cc <= rr)
            ss.append(jnp.where(ok, s, NEG))
            vs.append(winv_ref[pl.ds(k0c, BAND), :])
        m = functools.reduce(jnp.maximum, [jnp.max(s, axis=-1, keepdims=True) for s in ss])
        es = [jnp.exp(s - m) for s in ss]
        l = functools.reduce(jnp.add, [jnp.sum(e, axis=-1, keepdims=True) for e in es])
        acc = functools.reduce(jnp.add, [_dot(e.astype(BF16), v) for e, v in zip(es, vs)])
        o = acc / l
        o_win += [o[r * tq:(r + 1) * tq] for r in range(REP)]

    gates = gates_ref[...]
    mixed = []
    for hh in range(N_HEADS):
        gc = [gates[:, br * N_HEADS + hh:br * N_HEADS + hh + 1] for br in range(3)]
        mixed.append(gc[0] * o_cmp[hh] + gc[1] * o_sel[hh] + gc[2] * o_win[hh])
    o_ref[...] = _compact_heads(mixed).astype(BF16)


def _nsa_prompt(q, gates, comp, selk, selv, wink, winv, mwt, nb, seq):
    tq = BAND
    nq = seq // tq
    n = q.shape[0]
    n_c = comp.shape[1]
    row = lambda wd: pl.BlockSpec((tq, wd), lambda b_, i: (b_ * nq + i, 0))
    full = lambda wd: pl.BlockSpec((seq, wd), lambda b_, i: (b_, 0))
    return pl.pallas_call(
        functools.partial(_nsa_kernel, tq=tq),
        grid=(nb, nq),
        in_specs=[row(QPAD_W), row(LANES), pl.BlockSpec((1, n_c, KV_W), lambda b_, i: (b_, 0, 0)),
                  full(2 * LANES), full(LANES), full(2 * LANES), full(LANES), _const_spec(mwt.shape)],
        out_specs=row(N_HEADS * HEAD_DIM),
        out_shape=jax.ShapeDtypeStruct((n, N_HEADS * HEAD_DIM), BF16),
        compiler_params=_params(2),
        name="nsa_prompt",
    )(q, gates, comp, selk, selv, wink, winv, mwt)


MOBA_LANES = 16


def _moba_kernel(q_ref, rows_ref, k_ref, v_ref, o_ref, kmean_ref, *, tq, n_blk):
    i = pl.program_id(1)

    @pl.when(i == 0)
    def _():
        for j in range(n_blk):
            kmean_ref[j:j + 1, :] = jnp.sum(rows_ref[j * MOBA_BLOCK:(j + 1) * MOBA_BLOCK, :LANES],
                                            axis=0, keepdims=True) * (1.0 / MOBA_BLOCK)

    q = q_ref[...]
    km = kmean_ref[...]
    km_g = (km, pltpu.roll(km, HEAD_DIM, 1))
    jj = _iota((n_blk, tq), 0)
    lane = _iota((tq, LANES), 1)
    cc = _iota((1, MOBA_BLOCK), 1)
    rr = _iota((REP * tq, 1), 0) % tq

    def head(hh):
        return q[:, hh * LANES:(hh + 1) * LANES]

    outs = []
    for g in range(N_KV):
        ns_rows = [jnp.zeros((HEAD_DIM, tq), F32)]
        for r in range(REP):
            gate = jnp.where(jj < i, _dot_nt(km_g[g], head(g * REP + r), HI), NEG)
            sel = (_rank_rows(gate, n_blk) < MOBA_TOPK) & (jj < i)
            ns_rows.append(jnp.where(sel, 0.0, 1.0))
            if n_blk < MOBA_LANES:
                ns_rows.append(jnp.zeros((MOBA_LANES - n_blk, tq), F32))
        ns_rows.append(jnp.zeros((HEAD_DIM - REP * MOBA_LANES, tq), F32))
        ns = jnp.concatenate(ns_rows, axis=0).T
        qa, qo = [], []
        for r in range(REP):
            mine = (lane >= HEAD_DIM + r * MOBA_LANES) & (lane < HEAD_DIM + (r + 1) * MOBA_LANES)
            qh = head(g * REP + r).astype(BF16)
            qo.append(qh)
            qa.append(qh + jnp.where(mine, ns, 0.0).astype(BF16))
        qa = jnp.concatenate(qa, axis=0)
        qo = jnp.concatenate(qo, axis=0)

        def body(kt, carry, g=g, qa=qa):
            k0 = pl.multiple_of(kt * MOBA_BLOCK, MOBA_BLOCK)
            s = _dot_nt(qa, k_ref[pl.ds(k0, MOBA_BLOCK), g * LANES:(g + 1) * LANES])
            return _flash_step(carry, s, v_ref[pl.ds(k0, MOBA_BLOCK), :])

        carry = lax.fori_loop(0, i, body, _flash_init(REP * tq))
        k0 = pl.multiple_of(i * MOBA_BLOCK, MOBA_BLOCK)
        s = _dot_nt(qo, k_ref[pl.ds(k0, MOBA_BLOCK), g * LANES:(g + 1) * LANES])
        m, l, acc = _flash_step(carry, jnp.where(cc <= rr, s, NEG), v_ref[pl.ds(k0, MOBA_BLOCK), :])
        o = acc / l
        outs += [o[r * tq:(r + 1) * tq] for r in range(REP)]
    o_ref[...] = _compact_heads(outs).astype(BF16)


def _moba_prompt(qm, rows, mobak, mobav, nb, seq):
    tq = MOBA_BLOCK
    nq = seq // tq
    n = qm.shape[0]
    full = lambda wd: pl.BlockSpec((seq, wd), lambda b_, i: (b_, 0))
    return pl.pallas_call(
        functools.partial(_moba_kernel, tq=tq, n_blk=nq),
        grid=(nb, nq),
        in_specs=[pl.BlockSpec((tq, QPAD_W), lambda b_, i: (b_ * nq + i, 0)),
                  full(KV_W), full(2 * LANES), full(LANES)],
        out_specs=pl.BlockSpec((tq, N_HEADS * HEAD_DIM), lambda b_, i: (b_ * nq + i, 0)),
        out_shape=jax.ShapeDtypeStruct((n, N_HEADS * HEAD_DIM), BF16),
        scratch_shapes=[pltpu.VMEM((nq, LANES), F32)],
        compiler_params=_params(2),
        name="moba_prompt",
    )(qm, rows, mobak, mobav)


FFN_CHUNK = 1408


def _ffn_kernel(x_ref, conv_ref, nsa_ref, moba_ref, wo_ref, g_ref, wgu_ref, wd_ref, o_ref):
    mix = jnp.concatenate([conv_ref[...], nsa_ref[...], moba_ref[...]], axis=1)
    x1 = x_ref[...] + _dot(mix, wo_ref[...])
    h2 = _rms(x1, g_ref[...]).astype(BF16)
    acc = x1
    for c0 in range(0, FFN_HIDDEN, FFN_CHUNK):
        gt = _dot(h2, wgu_ref[:, c0:c0 + FFN_CHUNK])
        up = _dot(h2, wgu_ref[:, FFN_HIDDEN + c0:FFN_HIDDEN + c0 + FFN_CHUNK])
        acc = acc + _dot((gt * jax.nn.sigmoid(gt) * up).astype(BF16), wd_ref[c0:c0 + FFN_CHUNK, :])
    o_ref[...] = acc


def _ffn(x, conv_o, nsa_o, moba_o, wo, g, wgu, wd, tm):
    n = x.shape[0]
    row = lambda wd_: pl.BlockSpec((tm, wd_), lambda i: (i, 0))
    return pl.pallas_call(
        _ffn_kernel,
        grid=(n // tm,),
        in_specs=[row(D_MODEL), row(CONV_CH), row(N_HEADS * HEAD_DIM), row(N_HEADS * HEAD_DIM),
                  _const_spec((D_MODEL, D_MODEL)), _const_spec((1, D_MODEL)),
                  _const_spec((D_MODEL, 2 * FFN_HIDDEN)), _const_spec((FFN_HIDDEN, D_MODEL))],
        out_specs=row(D_MODEL),
        out_shape=jax.ShapeDtypeStruct((n, D_MODEL), F32),
        compiler_params=_params(1),
        name="mix_ffn",
    )(x, conv_o, nsa_o, moba_o, wo, g, wgu, wd)


def _norm_kernel(x_ref, g_ref, o_ref):
    o_ref[...] = _rms(x_ref[...], g_ref[...])


def _final_norm(x, g, tm):
    n = x.shape[0]
    return pl.pallas_call(
        _norm_kernel,
        grid=(n // tm,),
        in_specs=[pl.BlockSpec((tm, D_MODEL), lambda i: (i, 0)), _const_spec((1, D_MODEL))],
        out_specs=pl.BlockSpec((tm, D_MODEL), lambda i: (i, 0)),
        out_shape=jax.ShapeDtypeStruct((n, D_MODEL), F32),
        compiler_params=_params(1),
        name="final_norm",
    )(x, g)


def _q8(q, g):
    row = _iota((8, LANES), 0)
    out = jnp.zeros((8, LANES), F32)
    for r in range(REP):
        hh = g * REP + r
        out = jnp.where(row == r, q[:, hh * LANES:(hh + 1) * LANES], out)
    return pltpu.roll(out, HEAD_DIM, 1) if g == 1 else out


def _gather_pages(cache_ref, pt_ref, b, dst_ref, slot, sem, n_pages, rows_per_page, start):
    def body(p, _):
        cp = pltpu.make_async_copy(cache_ref.at[pt_ref[b, p]],
                                   dst_ref.at[slot, pl.ds(pl.multiple_of(p * rows_per_page, 8), rows_per_page)],
                                   sem.at[slot])
        if start:
            cp.start()
        else:
            cp.wait()
        return 0
    lax.fori_loop(0, n_pages, body, 0)


def _paged_prologue(cache_ref, pt_ref, dst_ref, sem, n_pages, rows_per_page):
    b = pl.program_id(0)
    slot = b % 2

    @pl.when(b == 0)
    def _():
        _gather_pages(cache_ref, pt_ref, 0, dst_ref, 0, sem, n_pages, rows_per_page, True)

    _gather_pages(cache_ref, pt_ref, b, dst_ref, slot, sem, n_pages, rows_per_page, False)

    @pl.when(b + 1 < pl.num_programs(0))
    def _():
        _gather_pages(cache_ref, pt_ref, b + 1, dst_ref, 1 - slot, sem, n_pages, rows_per_page, True)

    return slot


def _rank_lanes(score):
    n = score.shape[1]
    s_row = jnp.broadcast_to(score, (n, n))
    s_col = s_row.T
    ii = _iota((n, n), 0)
    jj = _iota((n, n), 1)
    beats = (s_col > s_row) | ((s_col == s_row) & (ii < jj))
    return jnp.sum(jnp.where(beats, 1, 0), axis=0, keepdims=True)


def _cmp_sample_kernel(pt_ref, cache_ref, new_ref, q_ref, wbig_ref, w2_ref, pe_ref, mw_ref,
                       o_ref, idx_ref, x_ref, xb_ref, ab_ref, sem, *, n_pages, past):
    slot = _paged_prologue(cache_ref, pt_ref, x_ref, sem, n_pages, 8)
    n_tok = n_pages * 8
    xb_ref[pl.ds(0, n_tok), :] = x_ref[slot].astype(BF16)
    new_chunk = jnp.concatenate([new_ref[0], jnp.zeros((1, CHUNK_W - KV_W), F32)], axis=1)
    xb_ref[pl.ds(n_tok, CMP_TAIL), :] = jnp.where(_iota((CMP_TAIL, 1), 0) == 0, new_chunk, 0.0).astype(BF16)
    comp = _compress(xb_ref[...], n_tok, wbig_ref, w2_ref, pe_ref, ab_ref)
    kc = comp[:, :LANES]
    vc = comp[:, LANES:].astype(BF16)
    q = q_ref[0]
    cmask = (_iota((1, n_tok), 1) * CMP_STRIDE + (2 * CMP_STRIDE - 1)) <= past
    row8 = _iota((8, 1), 0)
    n_sb = past // SEL_BLOCK + 1
    cur = past // SEL_BLOCK
    n_l = mw_ref.shape[1]
    jl = _iota((1, n_l), 1)
    for g in range(N_KV):
        p = _softmax_rows(_dot_nt(_q8(q, g), kc, HI), cmask)
        o_ref[0, g] = _dot(p.astype(BF16), vc)
        imp = jnp.sum(jnp.where(row8 < REP, p, 0.0), axis=0, keepdims=True)
        pslc = _dot(jnp.broadcast_to(imp, (8, n_tok)), mw_ref[...], HI)[0:1, :]
        forced = (jl == 0) | (jl == cur) | (jl == cur - 1)
        score = jnp.where(jl >= n_sb, -jnp.inf, jnp.where(forced, BIG, jnp.where(jl <= cur, pslc, NEG)))
        rank = _rank_lanes(score)
        kk = _iota((N_SEL, n_l), 0)
        idx = jnp.sum(jnp.where(rank == kk, _iota((N_SEL, n_l), 1), 0), axis=1, keepdims=True)
        idx_ref[0, g] = jnp.broadcast_to(idx, (N_SEL, LANES))


def _cmp_sample(pt, cache, new, q, wbig, w2, pe, mw, past):
    bd, n_pages = pt.shape
    n_tok = n_pages * 8
    blk = lambda *s: pl.BlockSpec((1,) + s, lambda b_, pt_: (b_,) + (0,) * len(s))
    cst = lambda shape: pl.BlockSpec(shape, lambda b_, pt_: (0,) * len(shape), pipeline_mode=pl.Buffered(1))
    gs = pltpu.PrefetchScalarGridSpec(
        num_scalar_prefetch=1, grid=(bd,),
        in_specs=[pl.BlockSpec(memory_space=pl.ANY), blk(1, KV_W), blk(1, QPAD_W),
                  cst((CHUNK_W, 8 * CMP_HIDDEN)), cst((4 * CMP_HIDDEN, KV_W)), cst((8, CHUNK_W)), cst(mw.shape)],
        out_specs=[blk(N_KV, 8, LANES), blk(N_KV, N_SEL, LANES)],
        scratch_shapes=[pltpu.VMEM((2, n_tok, CHUNK_W), F32), pltpu.VMEM((n_tok + CMP_TAIL, CHUNK_W), BF16),
                        pltpu.VMEM((n_tok + CMP_TAIL, 8 * CMP_HIDDEN), F32), pltpu.SemaphoreType.DMA((2,))])
    return pl.pallas_call(
        functools.partial(_cmp_sample_kernel, n_pages=n_pages, past=past),
        grid_spec=gs,
        out_shape=[jax.ShapeDtypeStruct((bd, N_KV, 8, LANES), F32), jax.ShapeDtypeStruct((bd, N_KV, N_SEL, LANES), I32)],
        compiler_params=_params(1, 60 * 1024 * 1024),
        name="cmp_sample",
    )(pt, cache.reshape(cache.shape[0], 8, CHUNK_W), new, q, wbig, w2, pe, mw)


def _one_query_attention(q8, k_bf, v_bf, kmask, k_new, v_new):
    s = _dot_nt(q8.astype(BF16), k_bf)
    if kmask is not None:
        s = jnp.where(kmask, s, NEG)
    s_new = jnp.sum(q8 * k_new, axis=-1, keepdims=True)
    m = jnp.maximum(jnp.max(s, axis=-1, keepdims=True), s_new)
    e = jnp.exp(s - m)
    e_new = jnp.exp(s_new - m)
    l = jnp.sum(e, axis=-1, keepdims=True) + e_new
    return (_dot(e.astype(BF16), v_bf) + e_new * v_new) / l


def _selwin_sample_kernel(pt_ref, idx_ref, cache_ref, newsel_ref, win_ref, newwin_ref, q_ref,
                          osel_ref, owin_ref, buf_ref, sem, *, n_cb):
    b = pl.program_id(0)

    def copy(g, k):
        blk = jnp.minimum(idx_ref[b, g * N_SEL + k], n_cb - 1)
        return pltpu.make_async_copy(cache_ref.at[pt_ref[b, blk // 2] * 2 + blk % 2], buf_ref.at[g, k], sem.at[0])

    for g in range(N_KV):
        for k in range(N_SEL):
            copy(g, k).start()
    q = q_ref[0]
    new_sel = newsel_ref[0]
    new_win = newwin_ref[0]
    win = win_ref[0]
    for g in range(N_KV):
        owin_ref[0, g] = _one_query_attention(_q8(q, g), win[:, :LANES].astype(BF16), win[:, LANES:].astype(BF16),
                                              None, new_win[:, :LANES], new_win[:, LANES:])
    for g in range(N_KV):
        for k in range(N_SEL):
            copy(g, k).wait()
    lane_blk = _iota((1, N_SEL * SEL_BLOCK), 1) // SEL_BLOCK
    for g in range(N_KV):
        valid = jnp.zeros((1, N_SEL * SEL_BLOCK), I32)
        for k in range(N_SEL):
            valid = jnp.where(lane_blk == k, (idx_ref[b, g * N_SEL + k] < n_cb).astype(I32), valid)
        rows = buf_ref[g].reshape(N_SEL * SEL_BLOCK, KV_W)
        osel_ref[0, g] = _one_query_attention(_q8(q, g), rows[:, :LANES].astype(BF16), rows[:, LANES:].astype(BF16),
                                              valid > 0, new_sel[:, :LANES], new_sel[:, LANES:])


def _selwin_sample(pt, idx, cache, newsel, win, newwin, q, past):
    bd = pt.shape[0]
    wb = win.shape[1]
    blk = lambda *s: pl.BlockSpec((1,) + s, lambda b_, *_: (b_,) + (0,) * len(s))
    gs = pltpu.PrefetchScalarGridSpec(
        num_scalar_prefetch=2, grid=(bd,),
        in_specs=[pl.BlockSpec(memory_space=pl.ANY), blk(1, KV_W), blk(wb, KV_W), blk(1, KV_W), blk(1, QPAD_W)],
        out_specs=[blk(N_KV, 8, LANES), blk(N_KV, 8, LANES)],
        scratch_shapes=[pltpu.VMEM((N_KV, N_SEL, SEL_BLOCK, KV_W), F32), pltpu.SemaphoreType.DMA((1,))])
    return pl.pallas_call(
        functools.partial(_selwin_sample_kernel, n_cb=past // SEL_BLOCK),
        grid_spec=gs,
        out_shape=[jax.ShapeDtypeStruct((bd, N_KV, 8, LANES), F32)] * 2,
        compiler_params=_params(1),
        name="selwin_sample",
    )(pt, idx, cache.reshape(cache.shape[0] * 2, SEL_BLOCK, KV_W), newsel, win, newwin, q)


def _moba_sample_kernel(pt_ref, cache_ref, new_ref, q_ref, o_ref, x_ref, s_ref, km_ref, sem, *, n_pages, n_blk):
    slot = _paged_prologue(cache_ref, pt_ref, x_ref, sem, n_pages, PAGE)
    q = q_ref[0]
    new = new_ref[0]
    km_ref[...] = jnp.zeros((LANES, LANES), F32)
    for j in range(n_blk):
        km_ref[j:j + 1, :] = jnp.sum(x_ref[slot, j * MOBA_BLOCK:(j + 1) * MOBA_BLOCK, :LANES],
                                     axis=0, keepdims=True) * (1.0 / MOBA_BLOCK)
    km = km_ref[...]
    jl = _iota((1, LANES), 1)
    row8 = _iota((8, LANES), 0)
    for g in range(N_KV):
        q8 = _q8(q, g)
        gate = _dot_nt(q8, km, HI)
        notsel = jnp.ones((8, LANES), F32)
        for r in range(REP):
            score = jnp.where(jl < n_blk, gate[r:r + 1, :], -jnp.inf)
            sel = (_rank_lanes(score) < MOBA_TOPK) & (jl < n_blk)
            notsel = jnp.where((row8 == r) & sel, 0.0, notsel)
        q8b = q8.astype(BF16)
        for j in range(n_blk):
            kj = x_ref[slot, j * MOBA_BLOCK:(j + 1) * MOBA_BLOCK, :LANES].astype(BF16)
            s_ref[:, j * MOBA_BLOCK:(j + 1) * MOBA_BLOCK] = jnp.where(notsel[:, j:j + 1] > 0.5, NEG, _dot_nt(q8b, kj))
        s = s_ref[...]
        s_new = jnp.sum(q8 * new[:, :LANES], axis=-1, keepdims=True)
        m = jnp.maximum(jnp.max(s, axis=-1, keepdims=True), s_new)
        e = jnp.exp(s - m)
        e_new = jnp.exp(s_new - m)
        l = jnp.sum(e, axis=-1, keepdims=True) + e_new
        o_ref[0, g] = (_dot(e.astype(BF16), x_ref[slot, :, LANES:].astype(BF16)) + e_new * new[:, LANES:]) / l


def _moba_sample(pt, cache, new, q, past):
    bd, n_pages = pt.shape
    n_blk = past // MOBA_BLOCK
    blk = lambda *s: pl.BlockSpec((1,) + s, lambda b_, pt_: (b_,) + (0,) * len(s))
    gs = pltpu.PrefetchScalarGridSpec(
        num_scalar_prefetch=1, grid=(bd,),
        in_specs=[pl.BlockSpec(memory_space=pl.ANY), blk(1, KV_W), blk(1, QPAD_W)],
        out_specs=blk(N_KV, 8, LANES),
        scratch_shapes=[pltpu.VMEM((2, past, KV_W), F32), pltpu.VMEM((8, past), F32), pltpu.VMEM((LANES, LANES), F32),
                        pltpu.SemaphoreType.DMA((2,))])
    return pl.pallas_call(
        functools.partial(_moba_sample_kernel, n_pages=n_pages, n_blk=n_blk),
        grid_spec=gs,
        out_shape=jax.ShapeDtypeStruct((bd, N_KV, 8, LANES), F32),
        compiler_params=_params(1),
        name="moba_sample",
    )(pt, cache, new, q)


def _combine_sample_kernel(u_ref, st_ref, w_ref, b_ref, lg_ref, lb_ref, gates_ref, ocmp_ref, osel_ref, owin_ref,
                           omoba_ref, conv_ref, nsa_ref, moba_ref):
    y = u_ref[...] * w_ref[CONV_WIDTH - 1:CONV_WIDTH, :]
    for j in range(CONV_WIDTH - 1):
        y = y + st_ref[j] * w_ref[j:j + 1, :]
    conv_ref[...] = _conv_post(y, b_ref, lg_ref, lb_ref).astype(BF16)
    gates = gates_ref[...]
    w = N_HEADS * HEAD_DIM
    head_of_lane = _iota((1, w), 1) // HEAD_DIM
    acc = jnp.zeros(ocmp_ref.shape, F32)
    for br, ref in enumerate((ocmp_ref, osel_ref, owin_ref)):
        ge = jnp.zeros(ocmp_ref.shape, F32)
        for hh in range(N_HEADS):
            c = br * N_HEADS + hh
            ge = jnp.where(head_of_lane == hh, gates[:, c:c + 1], ge)
        acc = acc + ge * ref[...]
    nsa_ref[...] = acc.astype(BF16)
    moba_ref[...] = omoba_ref[...].astype(BF16)


def _combine_sample(u, st_t, w, b, lg, lb, gates, o_cmp, o_sel, o_win, o_moba):
    bd = u.shape[0]
    w_h = N_HEADS * HEAD_DIM
    return pl.pallas_call(
        _combine_sample_kernel,
        out_shape=[jax.ShapeDtypeStruct((bd, CONV_CH), BF16), jax.ShapeDtypeStruct((bd, w_h), BF16),
                   jax.ShapeDtypeStruct((bd, w_h), BF16)],
        name="combine_sample",
    )(u, st_t, w, b, lg, lb, gates, o_cmp, o_sel, o_win, o_moba)


def _rope_tables(pos):
    half = HEAD_DIM // 2
    inv = ROPE_THETA ** (-jnp.arange(half, dtype=F32) / half)
    ang = pos.astype(F32)[:, None] * inv[None, :]
    cos, sin = jnp.cos(ang), jnp.sin(ang)
    zero = jnp.zeros_like(sin)
    return (jnp.tile(cos, (1, 4)), jnp.tile(jnp.concatenate([-sin, zero], axis=1), (1, 2)),
            jnp.tile(jnp.concatenate([zero, sin], axis=1), (1, 2)))


def _pad_heads(w):
    d = w.shape[:-1]
    w = w.reshape(*d, N_HEADS, HEAD_DIM)
    return jnp.pad(w, [(0, 0)] * len(d) + [(0, 0), (0, LANES - HEAD_DIM)]).reshape(*d, QPAD_W)


def _relayout_w_in(w_in):
    c_q = 2 * CONV_CH
    c_kv = c_q + N_HEADS * HEAD_DIM
    c_g = c_kv + 6 * N_KV * HEAD_DIM
    c_qm = c_g + 3 * N_HEADS
    c_kvm = c_qm + N_HEADS * HEAD_DIM
    gates = jnp.pad(w_in[..., c_g:c_qm], ((0, 0), (0, 0), (0, LANES - 3 * N_HEADS)))
    return jnp.concatenate([w_in[..., :c_q], _pad_heads(w_in[..., c_q:c_kv]), w_in[..., c_kv:c_g],
                            _pad_heads(w_in[..., c_qm:c_kvm]), w_in[..., c_kvm:], gates], axis=-1).astype(BF16)


def _compress_weights(pe_k, pe_v, wk1, wk2, wv1, wv2):
    depth = wk1.shape[0]
    eye = jnp.eye(4, dtype=F32)
    w1 = jnp.stack([wk1, wk1, wv1, wv1], axis=1).reshape(depth, 4, 2, CMP_STRIDE, HEAD_DIM, CMP_HIDDEN)
    wbig = jnp.einsum('lsaidh,st->lisdath', w1, eye).reshape(depth, CHUNK_W, 8 * CMP_HIDDEN).astype(BF16)
    w2 = jnp.stack([wk2, wk2, wv2, wv2], axis=1)
    w2big = jnp.einsum('lshd,st->lshtd', w2, eye).reshape(depth, 4 * CMP_HIDDEN, KV_W).astype(BF16)
    pe = jnp.stack([pe_k, pe_k, pe_v, pe_v], axis=1).reshape(depth, 4, 2, CMP_STRIDE, HEAD_DIM)
    pe = pe.transpose(0, 2, 3, 1, 4).reshape(depth, 2, CHUNK_W)
    return wbig, w2big, jnp.pad(pe, ((0, 0), (0, 6), (0, 0)))


def _slc_matrix(n_c, n_sb):
    ratio = SEL_BLOCK // CMP_STRIDE
    c = np.arange(n_c)[:, None]
    j = np.arange(n_sb)[None, :]
    m = np.where(c == ratio * j, 1.0, 0.0) + np.where((c > ratio * j) & (c < ratio * (j + 1)), 2.0, 0.0) \
        + np.where(c == ratio * (j + 1), 1.0, 0.0)
    return m.astype(np.float32)


def _heads_from_q8(o):
    return jnp.concatenate([o[:, 0, :REP, :HEAD_DIM], o[:, 1, :REP, HEAD_DIM:]], axis=1).reshape(o.shape[0], -1)


def kernel(x_prompt, x_sample, cache_nsa_cmp, cache_nsa_sel, cache_moba, state_nsa_win, state_conv, page_table,
           g_mix, w_in, conv_w, conv_b, conv_ln_g, conv_ln_b, cmp_pe_k, cmp_pe_v, cmp_wk1, cmp_wk2, cmp_wv1, cmp_wv2,
           w_out, g_ffn, w_gate_up, w_down, g_final):
    nb, seq, _ = x_prompt.shape
    bd, dec_seq, _ = x_sample.shape
    depth = g_mix.shape[0]
    n_pages = page_table.shape[1]
    past = n_pages * PAGE
    assert dec_seq == 1 and seq % 512 == 0 and seq // SEL_BLOCK <= HEAD_DIM and 3 * (seq // MOBA_BLOCK) <= HEAD_DIM
    assert past % MOBA_BLOCK == 0 and past // MOBA_BLOCK <= LANES and bd % 8 == 0
    n_p = nb * seq
    tm = 512

    w_in_r = _relayout_w_in(w_in)
    wbig, w2big, pe_rows = _compress_weights(cmp_pe_k, cmp_pe_v, cmp_wk1, cmp_wk2, cmp_wv1, cmp_wv2)
    w_out_b, w_gu_b, w_down_b = w_out.astype(BF16), w_gate_up.astype(BF16), w_down.astype(BF16)
    conv_w_p = jnp.pad(conv_w, ((0, 0), (0, CONV_HALO - CONV_WIDTH), (0, 0)))
    tab_p = _rope_tables(jnp.arange(seq))
    tab_s = _rope_tables(jnp.full((bd,), past))
    n_c = seq // CMP_STRIDE
    mwt_p = jnp.asarray(_slc_matrix(n_c, seq // SEL_BLOCK).T)
    n_sb_s = past // SEL_BLOCK + 1
    mw_s = jnp.asarray(np.pad(_slc_matrix(past // CMP_STRIDE, n_sb_s), ((0, 0), (0, -n_sb_s % LANES))))

    xp = x_prompt.reshape(n_p, D_MODEL)
    xs = x_sample.reshape(bd, D_MODEL)
    new_p = [[], [], [], [], []]
    new_s = [[], [], [], [], []]
    r2 = lambda a: a.reshape(1, -1)
    for l in range(depth):
        (u, q, cmp_rows, sel_rows, win_rows, moba_rows, qm, gates,
         selk, selv, wink, winv, mobak, mobav) = _proj(xp, r2(g_mix[l]), w_in_r[l], *tab_p, seq, tm)
        conv_o = _conv(u, conv_w_p[l], r2(conv_b[l]), r2(conv_ln_g[l]), r2(conv_ln_b[l]), nb, seq, 512)
        comp = _compress_prompt(cmp_rows, wbig[l], w2big[l], pe_rows[l], nb, seq)
        nsa_o = _nsa_prompt(q, gates, comp, selk, selv, wink, winv, mwt_p, nb, seq)
        moba_o = _moba_prompt(qm, moba_rows, mobak, mobav, nb, seq)
        xp = _ffn(xp, conv_o, nsa_o, moba_o, w_out_b[l], r2(g_ffn[l]), w_gu_b[l], w_down_b[l], tm)
        kv6 = lambda a, n_: a.reshape(n_, -1, 2, N_KV, HEAD_DIM)
        new_p[0].append(kv6(cmp_rows, nb))
        new_p[1].append(kv6(sel_rows, nb))
        new_p[2].append(kv6(moba_rows, nb))
        new_p[3].append(kv6(win_rows, nb)[:, -min(WINDOW, seq):])
        new_p[4].append(u.reshape(nb, seq, CONV_CH)[:, -(CONV_WIDTH - 1):])

        (u, q, cmp_rows, sel_rows, win_rows, moba_rows, qm, gates, *_) = _proj(xs, r2(g_mix[l]), w_in_r[l], *tab_s, bd, bd)
        b3 = lambda a: a.reshape(bd, 1, -1)
        o_cmp, idx = _cmp_sample(page_table, cache_nsa_cmp[l], b3(cmp_rows), b3(q), wbig[l], w2big[l], pe_rows[l],
                                 mw_s, past)
        win_buf = state_nsa_win[l].reshape(bd, -1, KV_W)
        o_sel, o_win = _selwin_sample(page_table, idx[:, :, :, 0].reshape(bd, N_KV * N_SEL), cache_nsa_sel[l],
                                      b3(sel_rows), win_buf, b3(win_rows), b3(q), past)
        o_moba = _moba_sample(page_table, cache_moba[l].reshape(-1, PAGE, KV_W), b3(moba_rows), b3(qm), past)
        conv_o, nsa_o, moba_o = _combine_sample(
            u, state_conv[l].transpose(1, 0, 2), conv_w_p[l], r2(conv_b[l]), r2(conv_ln_g[l]), r2(conv_ln_b[l]), gates,
            _heads_from_q8(o_cmp), _heads_from_q8(o_sel), _heads_from_q8(o_win), _heads_from_q8(o_moba))
        xs = _ffn(xs, conv_o, nsa_o, moba_o, w_out_b[l], r2(g_ffn[l]), w_gu_b[l], w_down_b[l], bd)
        new_s[0].append(kv6(cmp_rows, bd))
        new_s[1].append(kv6(sel_rows, bd))
        new_s[2].append(kv6(moba_rows, bd))
        new_s[3].append(jnp.concatenate([state_nsa_win[l], kv6(win_rows, bd)], axis=1)[:, -state_nsa_win.shape[2]:])
        new_s[4].append(jnp.concatenate([state_conv[l], u[:, None, :]], axis=1)[:, -(CONV_WIDTH - 1):])

    y_prompt = _final_norm(xp, r2(g_final), tm).reshape(nb, seq, D_MODEL)
    y_sample = _final_norm(xs, r2(g_final), bd).reshape(bd, 1, D_MODEL)
    return (y_prompt, y_sample,
            jnp.stack(new_p[0]), jnp.stack(new_s[0]),
            jnp.stack(new_p[1]), jnp.stack(new_s[1]),
            jnp.stack(new_p[2]), jnp.stack(new_s[2]),
            jnp.stack(new_p[3]), jnp.stack(new_s[3]),
            jnp.stack(new_p[4]), jnp.stack(new_s[4]))
```

```python
import functools

import numpy as np
import jax
import jax.numpy as jnp
from jax import lax
from jax.experimental import pallas as pl
from jax.experimental.pallas import tpu as pltpu

F32 = jnp.float32
BF16 = jnp.bfloat16
I32 = jnp.int32

D_MODEL = 1024
HEAD_DIM = 64
LANES = 128
CONV_CH = 256
N_HEADS = 6
N_KV = 2
REP = N_HEADS // N_KV
CONV_WIDTH = 31
CMP_STRIDE = 16
CMP_HIDDEN = 128
SEL_BLOCK = 64
N_SEL = 16
WINDOW = 512
BAND = 128
MOBA_BLOCK = 256
MOBA_TOPK = 3
PAGE = 128
FFN_HIDDEN = 2816
ROPE_THETA = 10000.0
EPS = 1e-6
NEG = -1e30
BIG = 1e30
ATTN_SCALE = HEAD_DIM ** -0.5
HI = lax.Precision.HIGHEST

KV_W = 2 * N_KV * HEAD_DIM
QPAD_W = N_HEADS * LANES
CHUNK_W = CMP_STRIDE * KV_W

OFF_U, OFF_Q, OFF_KV, OFF_QM, OFF_KVM, OFF_G, W_IN_COLS = 0, 512, 1280, 2048, 2816, 3072, 3200

VMEM_LIMIT = 56 * 1024 * 1024


def _params(n_axes, limit=VMEM_LIMIT):
    return pltpu.CompilerParams(dimension_semantics=("arbitrary",) * n_axes, vmem_limit_bytes=limit)


def _const_spec(shape):
    nd = len(shape)
    return pl.BlockSpec(shape, lambda *_: (0,) * nd, pipeline_mode=pl.Buffered(1))


def _dot(a, b, precision=None):
    return jnp.dot(a, b, preferred_element_type=F32, precision=precision)


def _dot_nt(a, b, precision=None):
    return lax.dot_general(a, b, (((1,), (1,)), ((), ())), preferred_element_type=F32, precision=precision)


def _iota(shape, dim):
    return lax.broadcasted_iota(I32, shape, dim)


def _rms(x, g):
    return x * lax.rsqrt(jnp.mean(x * x, axis=-1, keepdims=True) + EPS) * g


def _compact_heads(a):
    lo = _iota(a[0].shape, 1) < HEAD_DIM
    return jnp.concatenate([
        jnp.where(lo, a[0], pltpu.roll(a[1], HEAD_DIM, 1)),
        jnp.where(lo, a[2], a[3]),
        jnp.where(lo, pltpu.roll(a[4], HEAD_DIM, 1), a[5])], axis=1)


def _proj_kernel(x_ref, g_ref, w_ref, cos_ref, sa_ref, sb_ref,
                 u_ref, q_ref, cmp_ref, sel_ref, win_ref, moba_ref, qm_ref, gates_ref,
                 selk_ref, selv_ref, wink_ref, winv_ref, mobak_ref, mobav_ref, *, seq, tm):
    i = pl.program_id(0)
    h = _rms(x_ref[...], g_ref[...]).astype(BF16)
    cos, sa, sb = cos_ref[...], sa_ref[...], sb_ref[...]

    def seg(off, width):
        return _dot(h, w_ref[:, off:off + width])

    def rope(z):
        return z * cos + pltpu.roll(z, LANES - 32, 1) * sa + pltpu.roll(z, 32, 1) * sb

    zu = seg(OFF_U, 2 * CONV_CH)
    u_ref[...] = zu[:, :CONV_CH] * jax.nn.sigmoid(zu[:, CONV_CH:])

    for off, ref in ((OFF_Q, q_ref), (OFF_QM, qm_ref)):
        z = seg(off, QPAD_W)
        for hh in range(N_HEADS):
            ref[:, hh * LANES:(hh + 1) * LANES] = rope(z[:, hh * LANES:(hh + 1) * LANES]) * ATTN_SCALE

    t = (i * tm) % seq + _iota((tm, LANES), 0)
    lane = _iota((tm, LANES), 1)
    lo = lane < HEAD_DIM
    oh_sel = jnp.where((lane >= HEAD_DIM) & ((t >> 6) == lane - HEAD_DIM), NEG, 0.0)
    oh_moba = jnp.where((lane >= HEAD_DIM) & (lane < HEAD_DIM + REP * MOBA_LANES)
                        & ((t >> 8) == ((lane - HEAD_DIM) & (MOBA_LANES - 1))), NEG, 0.0)

    def rows(off, ref):
        z = seg(off, KV_W)
        k = rope(z[:, :LANES])
        v = z[:, LANES:]
        ref[:, :LANES] = k
        ref[:, LANES:] = v
        return k, v

    rows(OFF_KV, cmp_ref)
    k, v = rows(OFF_KV + KV_W, sel_ref)
    selk_ref[:, :LANES] = jnp.where(lo, k, oh_sel).astype(BF16)
    selk_ref[:, LANES:] = jnp.where(lo, pltpu.roll(k, HEAD_DIM, 1), oh_sel).astype(BF16)
    selv_ref[...] = v.astype(BF16)
    k, v = rows(OFF_KV + 2 * KV_W, win_ref)
    wink_ref[:, :LANES] = k.astype(BF16)
    wink_ref[:, LANES:] = pltpu.roll(k, HEAD_DIM, 1).astype(BF16)
    winv_ref[...] = v.astype(BF16)
    k, v = rows(OFF_KVM, moba_ref)
    mobak_ref[:, :LANES] = jnp.where(lo, k, oh_moba).astype(BF16)
    mobak_ref[:, LANES:] = jnp.where(lo, pltpu.roll(k, HEAD_DIM, 1), oh_moba).astype(BF16)
    mobav_ref[...] = v.astype(BF16)

    gates_ref[...] = jax.nn.sigmoid(seg(OFF_G, LANES))


def _proj(x, g, w, cos, sa, sb, seq, tm):
    n = x.shape[0]
    nt = cos.shape[0] // tm
    row = lambda wd: pl.BlockSpec((tm, wd), lambda i: (i, 0))
    tab = pl.BlockSpec((tm, LANES), lambda i: (i % nt, 0))
    f32_w = (CONV_CH, QPAD_W, KV_W, KV_W, KV_W, KV_W, QPAD_W, LANES)
    bf_w = (2 * LANES, LANES, 2 * LANES, LANES, 2 * LANES, LANES)
    return pl.pallas_call(
        functools.partial(_proj_kernel, seq=seq, tm=tm),
        grid=(n // tm,),
        in_specs=[row(D_MODEL), _const_spec((1, D_MODEL)), _const_spec((D_MODEL, W_IN_COLS)), tab, tab, tab],
        out_specs=[row(wd) for wd in f32_w + bf_w],
        out_shape=[jax.ShapeDtypeStruct((n, wd), F32) for wd in f32_w]
        + [jax.ShapeDtypeStruct((n, wd), BF16) for wd in bf_w],
        compiler_params=_params(1),
        name="proj",
    )(x, g, w, cos, sa, sb)


CONV_HALO = 32


def _conv_post(y, b_ref, lg_ref, lb_ref):
    y = y + b_ref[...]
    mu = jnp.mean(y, axis=-1, keepdims=True)
    var = jnp.mean(jnp.square(y - mu), axis=-1, keepdims=True)
    y = (y - mu) * lax.rsqrt(var + EPS) * lg_ref[...] + lb_ref[...]
    return y * jax.nn.sigmoid(y)


def _conv_kernel(u_ref, prev_ref, w_ref, b_ref, lg_ref, lb_ref, o_ref, ext_ref, *, tq):
    i = pl.program_id(1)
    ext_ref[:CONV_HALO, :] = jnp.where(i > 0, prev_ref[...], 0.0)
    ext_ref[CONV_HALO:, :] = u_ref[...]
    off = CONV_HALO - (CONV_WIDTH - 1)
    y = jnp.zeros((tq, CONV_CH), F32)
    for j in range(CONV_WIDTH):
        y = y + ext_ref[pl.ds(off + j, tq), :] * w_ref[j:j + 1, :]
    o_ref[...] = _conv_post(y, b_ref, lg_ref, lb_ref).astype(BF16)


def _conv(u, w, b, lg, lb, nb, seq, tq):
    n = u.shape[0]
    nq = seq // tq
    per = tq // CONV_HALO
    return pl.pallas_call(
        functools.partial(_conv_kernel, tq=tq),
        grid=(nb, nq),
        in_specs=[pl.BlockSpec((tq, CONV_CH), lambda b_, i: (b_ * nq + i, 0)),
                  pl.BlockSpec((CONV_HALO, CONV_CH), lambda b_, i: (jnp.maximum((b_ * nq + i) * per - 1, 0), 0)),
                  _const_spec((CONV_HALO, CONV_CH)), _const_spec((1, CONV_CH)), _const_spec((1, CONV_CH)),
                  _const_spec((1, CONV_CH))],
        out_specs=pl.BlockSpec((tq, CONV_CH), lambda b_, i: (b_ * nq + i, 0)),
        out_shape=jax.ShapeDtypeStruct((n, CONV_CH), BF16),
        scratch_shapes=[pltpu.VMEM((tq + CONV_HALO, CONV_CH), F32)],
        compiler_params=_params(2),
        name="conv",
    )(u, u, w, b, lg, lb)


def _gelu_tanh(x):
    return x * (0.5 * (1.0 + jnp.tanh(0.7978845608028654 * (x + 0.044715 * (x * x * x)))))


CMP_TAIL = 16


def _compress(x_bf, n_tok, wbig_ref, w2_ref, pe_ref, ab_ref):
    half = 4 * CMP_HIDDEN
    ab_ref[...] = _dot(x_bf, wbig_ref[...])
    pw = _dot(pe_ref[...].astype(BF16), wbig_ref[...])
    pe_all = pw[0:1, :half] + pw[1:2, half:]
    hid = ab_ref[pl.ds(0, n_tok), :half] + ab_ref[pl.ds(1, n_tok), half:] + pe_all
    return _dot(_gelu_tanh(hid).astype(BF16), w2_ref[...])


def _compress_kernel(x_ref, wbig_ref, w2_ref, pe_ref, o_ref, x_scr, ab_ref, *, n_tok):
    x_scr[pl.ds(0, n_tok), :] = x_ref[0].astype(BF16)
    x_scr[pl.ds(n_tok, CMP_TAIL), :] = jnp.zeros((CMP_TAIL, CHUNK_W), BF16)
    o_ref[0] = _compress(x_scr[...], n_tok, wbig_ref, w2_ref, pe_ref, ab_ref)


def _compress_prompt(rows, wbig, w2, pe, nb, seq):
    n_tok = seq // CMP_STRIDE
    x = rows.reshape(nb, n_tok, CHUNK_W)
    return pl.pallas_call(
        functools.partial(_compress_kernel, n_tok=n_tok),
        grid=(nb,),
        in_specs=[pl.BlockSpec((1, n_tok, CHUNK_W), lambda b_: (b_, 0, 0)),
                  _const_spec((CHUNK_W, 8 * CMP_HIDDEN)), _const_spec((4 * CMP_HIDDEN, KV_W)),
                  _const_spec((8, CHUNK_W))],
        out_specs=pl.BlockSpec((1, n_tok, KV_W), lambda b_: (b_, 0, 0)),
        out_shape=jax.ShapeDtypeStruct((nb, n_tok, KV_W), F32),
        scratch_shapes=[pltpu.VMEM((n_tok + CMP_TAIL, CHUNK_W), BF16),
                        pltpu.VMEM((n_tok + CMP_TAIL, 8 * CMP_HIDDEN), F32)],
        compiler_params=_params(1),
        name="compress_prompt",
    )(x, wbig, w2, pe)


def _softmax_rows(s, mask):
    s = jnp.where(mask, s, NEG)
    m = jnp.max(s, axis=-1, keepdims=True)
    e = jnp.where(mask, jnp.exp(s - m), 0.0)
    l = jnp.sum(e, axis=-1, keepdims=True)
    return e / jnp.where(l > 0.0, l, 1.0)


def _rank_rows(score, n):
    jj = _iota(score.shape, 0)
    rank = jnp.zeros(score.shape, I32)
    for ii in range(n):
        row = score[ii:ii + 1, :]
        rank = rank + jnp.where((row > score) | ((row == score) & (ii < jj)), 1, 0)
    return rank


def _flash_step(carry, s, v):
    m, l, acc = carry
    m_new = jnp.maximum(m, jnp.max(s, axis=-1, keepdims=True))
    a = jnp.exp(m - m_new)
    p = jnp.exp(s - m_new)
    return m_new, a * l + jnp.sum(p, axis=-1, keepdims=True), a * acc + _dot(p.astype(BF16), v)


def _flash_init(rows):
    return jnp.full((rows, 1), NEG, F32), jnp.zeros((rows, 1), F32), jnp.zeros((rows, LANES), F32)


SEL_TK = 256


def _nsa_kernel(q_ref, gates_ref, comp_ref, selk_ref, selv_ref, wink_ref, winv_ref, mwt_ref, o_ref, *, tq):
    i = pl.program_id(1)
    t0 = i * tq
    q = q_ref[...]
    comp = comp_ref[0]
    n_c = comp.shape[0]
    kc = comp[:, :LANES]
    kc_g = (kc, pltpu.roll(kc, HEAD_DIM, 1))
    vc = comp[:, LANES:].astype(BF16)
    pos_col = t0 + _iota((tq, 1), 0)
    cmask = (_iota((1, n_c), 1) * CMP_STRIDE + (2 * CMP_STRIDE - 1)) <= pos_col
    n_sb = mwt_ref.shape[0]

    def head(hh):
        return q[:, hh * LANES:(hh + 1) * LANES]

    o_cmp, ns = [], []
    for g in range(N_KV):
        imp = jnp.zeros((tq, n_c), F32)
        for r in range(REP):
            p = _softmax_rows(_dot_nt(head(g * REP + r), kc_g[g], HI), cmask)
            imp = imp + p
            o_cmp.append(_dot(p.astype(BF16), vc))
        pslc_t = _dot_nt(mwt_ref[...], imp, HI)
        jj = _iota((n_sb, tq), 0)
        cur = (t0 + _iota((n_sb, tq), 1)) >> 6
        forced = (jj == 0) | (jj == cur) | (jj == cur - 1)
        score = jnp.where(forced, BIG, jnp.where(jj <= cur, pslc_t, NEG))
        notsel = jnp.where(_rank_rows(score, n_sb) < N_SEL, 0.0, 1.0)
        parts = [jnp.zeros((HEAD_DIM, tq), F32), notsel]
        if n_sb < HEAD_DIM:
            parts.append(jnp.zeros((HEAD_DIM - n_sb, tq), F32))
        ns.append(jnp.concatenate(parts, axis=0).T.astype(BF16))

    row_pos = t0 + (_iota((REP * tq, 1), 0) % tq)

    o_sel = []
    for g in range(N_KV):
        qa = jnp.concatenate([head(g * REP + r).astype(BF16) + ns[g] for r in range(REP)], axis=0)

        def tile(kt, g=g, qa=qa):
            k0 = pl.multiple_of(kt * SEL_TK, SEL_TK)
            s = _dot_nt(qa, selk_ref[pl.ds(k0, SEL_TK), g * LANES:(g + 1) * LANES])
            return s, selv_ref[pl.ds(k0, SEL_TK), :]

        def body(kt, carry):
            s, v = tile(kt)
            return _flash_step(carry, s, v)

        last = t0 // SEL_TK
        carry = lax.fori_loop(0, last, body, _flash_init(REP * tq))
        s, v = tile(last)
        kpos = last * SEL_TK + _iota((1, SEL_TK), 1)
        m, l, acc = _flash_step(carry, jnp.where(kpos <= row_pos, s, NEG), v)
        o = acc / l
        o_sel += [o[r * tq:(r + 1) * tq] for r in range(REP)]

    o_win = []
    n_w = (WINDOW + tq) // BAND
    cc = _iota((1, BAND), 1)
    rr = _iota((REP * tq, 1), 0) % tq
    for g in range(N_KV):
        qw = jnp.concatenate([head(g * REP + r).astype(BF16) for r in range(REP)], axis=0)
        ss, vs = [], []
        for j in range(n_w):
            k0 = t0 - WINDOW + j * BAND
            ok = k0 >= 0
            k0c = pl.multiple_of(jnp.maximum(k0, 0), BAND)
            s = _dot_nt(qw, wink_ref[pl.ds(k0c, BAND), g * LANES:(g + 1) * LANES])
            if BAND * j < tq:
                ok = ok & (cc >= rr - BAND * j)
            if WINDOW - BAND * j < BAND:
                ok = ok & (cc <= rr + (WINDOW - BAND * j))
            ss.append(jnp.where(ok, s, NEG))
            vs.append(winv_ref[pl.ds(k0c, BAND), :])
        m = functools.reduce(jnp.maximum, [jnp.max(s, axis=-1, keepdims=True) for s in ss])
        es = [jnp.exp(s - m) for s in ss]
        l = functools.reduce(jnp.add, [jnp.sum(e, axis=-1, keepdims=True) for e in es])
        acc = functools.reduce(jnp.add, [_dot(e.astype(BF16), v) for e, v in zip(es, vs)])
        o = acc / l
        o_win += [o[r * tq:(r + 1) * tq] for r in range(REP)]

    gates = gates_ref[...]
    mixed = []
    for hh in range(N_HEADS):
        gc = [gates[:, br * N_HEADS + hh:br * N_HEADS + hh + 1] for br in range(3)]
        mixed.append(gc[0] * o_cmp[hh] + gc[1] * o_sel[hh] + gc[2] * o_win[hh])
    o_ref[...] = _compact_heads(mixed).astype(BF16)


def _nsa_prompt(q, gates, comp, selk, selv, wink, winv, mwt, nb, seq):
    tq = SEL_TK
    nq = seq // tq
    n = q.shape[0]
    n_c = comp.shape[1]
    row = lambda wd: pl.BlockSpec((tq, wd), lambda b_, i: (b_ * nq + i, 0))
    full = lambda wd: pl.BlockSpec((seq, wd), lambda b_, i: (b_, 0))
    return pl.pallas_call(
        functools.partial(_nsa_kernel, tq=tq),
        grid=(nb, nq),
        in_specs=[row(QPAD_W), row(LANES), pl.BlockSpec((1, n_c, KV_W), lambda b_, i: (b_, 0, 0)),
                  full(2 * LANES), full(LANES), full(2 * LANES), full(LANES), _const_spec(mwt.shape)],
        out_specs=row(N_HEADS * HEAD_DIM),
        out_shape=jax.ShapeDtypeStruct((n, N_HEADS * HEAD_DIM), BF16),
        compiler_params=_params(2),
        name="nsa_prompt",
    )(q, gates, comp, selk, selv, wink, winv, mwt)


MOBA_LANES = 16


def _moba_kernel(q_ref, rows_ref, k_ref, v_ref, o_ref, kmean_ref, *, tq, n_blk):
    i = pl.program_id(1)

    @pl.when(i == 0)
    def _():
        for j in range(n_blk):
            kmean_ref[j:j + 1, :] = jnp.sum(rows_ref[j * MOBA_BLOCK:(j + 1) * MOBA_BLOCK, :LANES],
                                            axis=0, keepdims=True) * (1.0 / MOBA_BLOCK)

    q = q_ref[...]
    km = kmean_ref[...]
    km_g = (km, pltpu.roll(km, HEAD_DIM, 1))
    jj = _iota((n_blk, tq), 0)
    lane = _iota((tq, LANES), 1)
    cc = _iota((1, MOBA_BLOCK), 1)
    rr = _iota((REP * tq, 1), 0) % tq

    def head(hh):
        return q[:, hh * LANES:(hh + 1) * LANES]

    outs = []
    for g in range(N_KV):
        ns_rows = [jnp.zeros((HEAD_DIM, tq), F32)]
        for r in range(REP):
            gate = jnp.where(jj < i, _dot_nt(km_g[g], head(g * REP + r), HI), NEG)
            sel = (_rank_rows(gate, n_blk) < MOBA_TOPK) & (jj < i)
            ns_rows.append(jnp.where(sel, 0.0, 1.0))
            if n_blk < MOBA_LANES:
                ns_rows.append(jnp.zeros((MOBA_LANES - n_blk, tq), F32))
        ns_rows.append(jnp.zeros((HEAD_DIM - REP * MOBA_LANES, tq), F32))
        ns = jnp.concatenate(ns_rows, axis=0).T
        qa, qo = [], []
        for r in range(REP):
            mine = (lane >= HEAD_DIM + r * MOBA_LANES) & (lane < HEAD_DIM + (r + 1) * MOBA_LANES)
            qh = head(g * REP + r).astype(BF16)
            qo.append(qh)
            qa.append(qh + jnp.where(mine, ns, 0.0).astype(BF16))
        qa = jnp.concatenate(qa, axis=0)
        qo = jnp.concatenate(qo, axis=0)

        def body(kt, carry, g=g, qa=qa):
            k0 = pl.multiple_of(kt * MOBA_BLOCK, MOBA_BLOCK)
            s = _dot_nt(qa, k_ref[pl.ds(k0, MOBA_BLOCK), g * LANES:(g + 1) * LANES])
            return _flash_step(carry, s, v_ref[pl.ds(k0, MOBA_BLOCK), :])

        carry = lax.fori_loop(0, i, body, _flash_init(REP * tq))
        k0 = pl.multiple_of(i * MOBA_BLOCK, MOBA_BLOCK)
        s = _dot_nt(qo, k_ref[pl.ds(k0, MOBA_BLOCK), g * LANES:(g + 1) * LANES])
        m, l, acc = _flash_step(carry, jnp.where(cc <= rr, s, NEG), v_ref[pl.ds(k0, MOBA_BLOCK), :])
        o = acc / l
        outs += [o[r * tq:(r + 1) * tq] for r in range(REP)]
    o_ref[...] = _compact_heads(outs).astype(BF16)


def _moba_prompt(qm, rows, mobak, mobav, nb, seq):
    tq = MOBA_BLOCK
    nq = seq // tq
    n = qm.shape[0]
    full = lambda wd: pl.BlockSpec((seq, wd), lambda b_, i: (b_, 0))
    return pl.pallas_call(
        functools.partial(_moba_kernel, tq=tq, n_blk=nq),
        grid=(nb, nq),
        in_specs=[pl.BlockSpec((tq, QPAD_W), lambda b_, i: (b_ * nq + i, 0)),
                  full(KV_W), full(2 * LANES), full(LANES)],
        out_specs=pl.BlockSpec((tq, N_HEADS * HEAD_DIM), lambda b_, i: (b_ * nq + i, 0)),
        out_shape=jax.ShapeDtypeStruct((n, N_HEADS * HEAD_DIM), BF16),
        scratch_shapes=[pltpu.VMEM((nq, LANES), F32)],
        compiler_params=_params(2),
        name="moba_prompt",
    )(qm, rows, mobak, mobav)


FFN_CHUNK = 1408


def _ffn_kernel(x_ref, conv_ref, nsa_ref, moba_ref, wo_ref, g_ref, wgu_ref, wd_ref, o_ref):
    mix = jnp.concatenate([conv_ref[...], nsa_ref[...], moba_ref[...]], axis=1)
    x1 = x_ref[...] + _dot(mix, wo_ref[...])
    h2 = _rms(x1, g_ref[...]).astype(BF16)
    acc = x1
    for c0 in range(0, FFN_HIDDEN, FFN_CHUNK):
        gt = _dot(h2, wgu_ref[:, c0:c0 + FFN_CHUNK])
        up = _dot(h2, wgu_ref[:, FFN_HIDDEN + c0:FFN_HIDDEN + c0 + FFN_CHUNK])
        acc = acc + _dot((gt * jax.nn.sigmoid(gt) * up).astype(BF16), wd_ref[c0:c0 + FFN_CHUNK, :])
    o_ref[...] = acc


def _ffn(x, conv_o, nsa_o, moba_o, wo, g, wgu, wd, tm):
    n = x.shape[0]
    row = lambda wd_: pl.BlockSpec((tm, wd_), lambda i: (i, 0))
    return pl.pallas_call(
        _ffn_kernel,
        grid=(n // tm,),
        in_specs=[row(D_MODEL), row(CONV_CH), row(N_HEADS * HEAD_DIM), row(N_HEADS * HEAD_DIM),
                  _const_spec((D_MODEL, D_MODEL)), _const_spec((1, D_MODEL)),
                  _const_spec((D_MODEL, 2 * FFN_HIDDEN)), _const_spec((FFN_HIDDEN, D_MODEL))],
        out_specs=row(D_MODEL),
        out_shape=jax.ShapeDtypeStruct((n, D_MODEL), F32),
        compiler_params=_params(1),
        name="mix_ffn",
    )(x, conv_o, nsa_o, moba_o, wo, g, wgu, wd)


def _norm_kernel(x_ref, g_ref, o_ref):
    o_ref[...] = _rms(x_ref[...], g_ref[...])


def _final_norm(x, g, tm):
    n = x.shape[0]
    return pl.pallas_call(
        _norm_kernel,
        grid=(n // tm,),
        in_specs=[pl.BlockSpec((tm, D_MODEL), lambda i: (i, 0)), _const_spec((1, D_MODEL))],
        out_specs=pl.BlockSpec((tm, D_MODEL), lambda i: (i, 0)),
        out_shape=jax.ShapeDtypeStruct((n, D_MODEL), F32),
        compiler_params=_params(1),
        name="final_norm",
    )(x, g)


def _q8(q, g):
    row = _iota((8, LANES), 0)
    out = jnp.zeros((8, LANES), F32)
    for r in range(REP):
        hh = g * REP + r
        out = jnp.where(row == r, q[:, hh * LANES:(hh + 1) * LANES], out)
    return pltpu.roll(out, HEAD_DIM, 1) if g == 1 else out


def _gather_pages(page_src, page_dst, sem, n_pages, start):
    def body(p, _):
        cp = pltpu.make_async_copy(page_src(p), page_dst(p), sem)
        if start:
            cp.start()
        else:
            cp.wait()
        return 0
    lax.fori_loop(0, n_pages, body, 0)


def _paged_prologue(src_of, dst_of, sem, n_pages):
    b = pl.program_id(0)
    slot = b % 2

    def run(bb, sl, start):
        _gather_pages(lambda p: src_of(bb, p), lambda p: dst_of(sl, p), sem.at[sl], n_pages, start)

    @pl.when(b == 0)
    def _():
        run(0, 0, True)

    run(b, slot, False)

    @pl.when(b + 1 < pl.num_programs(0))
    def _():
        run(b + 1, 1 - slot, True)

    return slot


def _rank_lanes(score):
    n = score.shape[1]
    s_row = jnp.broadcast_to(score, (n, n))
    s_col = s_row.T
    ii = _iota((n, n), 0)
    jj = _iota((n, n), 1)
    beats = (s_col > s_row) | ((s_col == s_row) & (ii < jj))
    return jnp.sum(jnp.where(beats, 1, 0), axis=0, keepdims=True)


def _cmp_sample_kernel(pt_ref, cache_ref, new_ref, q_ref, wbig_ref, w2_ref, pe_ref, mw_ref,
                       o_ref, idx_ref, x_ref, rows_ref, xb_ref, ab_ref, sem, *, layer, n_pages, past):
    slot = _paged_prologue(lambda b, p: cache_ref.at[layer, pt_ref[b, p]], lambda sl, p: x_ref.at[sl, p], sem, n_pages)

    def to_rows(p, _):
        r0 = pl.multiple_of(p * PAGE, PAGE)
        for hv in range(2):
            rows_ref[hv, pl.ds(r0, PAGE), :] = x_ref[slot, p, hv * LANES:(hv + 1) * LANES, :].T
        return 0

    lax.fori_loop(0, n_pages, to_rows, 0)
    n_tok = n_pages * (PAGE // CMP_STRIDE)
    for i in range(CMP_STRIDE):
        for hv in range(2):
            c0 = i * KV_W + hv * LANES
            xb_ref[pl.ds(0, n_tok), c0:c0 + LANES] = rows_ref[hv, pl.ds(i, n_tok, stride=CMP_STRIDE), :].astype(BF16)
    new_chunk = jnp.concatenate([new_ref[0], jnp.zeros((1, CHUNK_W - KV_W), F32)], axis=1)
    xb_ref[pl.ds(n_tok, CMP_TAIL), :] = jnp.where(_iota((CMP_TAIL, 1), 0) == 0, new_chunk, 0.0).astype(BF16)
    comp = _compress(xb_ref[...], n_tok, wbig_ref, w2_ref, pe_ref, ab_ref)
    kc = comp[:, :LANES]
    vc = comp[:, LANES:].astype(BF16)
    q = q_ref[0]
    cmask = (_iota((1, n_tok), 1) * CMP_STRIDE + (2 * CMP_STRIDE - 1)) <= past
    row8 = _iota((8, 1), 0)
    n_sb = past // SEL_BLOCK + 1
    cur = past // SEL_BLOCK
    n_l = mw_ref.shape[1]
    jl = _iota((1, n_l), 1)
    for g in range(N_KV):
        p = _softmax_rows(_dot_nt(_q8(q, g), kc, HI), cmask)
        o_ref[0, g] = _dot(p.astype(BF16), vc)
        imp = jnp.sum(jnp.where(row8 < REP, p, 0.0), axis=0, keepdims=True)
        pslc = _dot(jnp.broadcast_to(imp, (8, n_tok)), mw_ref[...], HI)[0:1, :]
        forced = (jl == 0) | (jl == cur) | (jl == cur - 1)
        score = jnp.where(jl >= n_sb, -jnp.inf, jnp.where(forced, BIG, jnp.where(jl <= cur, pslc, NEG)))
        rank = _rank_lanes(score)
        kk = _iota((N_SEL, n_l), 0)
        idx = jnp.sum(jnp.where(rank == kk, _iota((N_SEL, n_l), 1), 0), axis=1, keepdims=True)
        idx_ref[0, g] = jnp.broadcast_to(idx, (N_SEL, LANES))


def _cmp_sample(layer, pt, cache_t, new, q, wbig, w2, pe, mw, past):
    bd, n_pages = pt.shape
    n_tok = n_pages * (PAGE // CMP_STRIDE)
    blk = lambda *s: pl.BlockSpec((1,) + s, lambda b_, pt_: (b_,) + (0,) * len(s))
    cst = lambda shape: pl.BlockSpec(shape, lambda b_, pt_: (0,) * len(shape), pipeline_mode=pl.Buffered(1))
    gs = pltpu.PrefetchScalarGridSpec(
        num_scalar_prefetch=1, grid=(bd,),
        in_specs=[pl.BlockSpec(memory_space=pl.ANY), blk(1, KV_W), blk(1, QPAD_W),
                  cst((CHUNK_W, 8 * CMP_HIDDEN)), cst((4 * CMP_HIDDEN, KV_W)), cst((8, CHUNK_W)), cst(mw.shape)],
        out_specs=[blk(N_KV, 8, LANES), blk(N_KV, N_SEL, LANES)],
        scratch_shapes=[pltpu.VMEM((2, n_pages, KV_W, PAGE), F32), pltpu.VMEM((2, n_pages * PAGE, LANES), F32),
                        pltpu.VMEM((n_tok + CMP_TAIL, CHUNK_W), BF16),
                        pltpu.VMEM((n_tok + CMP_TAIL, 8 * CMP_HIDDEN), F32), pltpu.SemaphoreType.DMA((2,))])
    return pl.pallas_call(
        functools.partial(_cmp_sample_kernel, layer=layer, n_pages=n_pages, past=past),
        grid_spec=gs,
        out_shape=[jax.ShapeDtypeStruct((bd, N_KV, 8, LANES), F32), jax.ShapeDtypeStruct((bd, N_KV, N_SEL, LANES), I32)],
        compiler_params=_params(1, 60 * 1024 * 1024),
        name="cmp_sample",
    )(pt, cache_t, new, q, wbig, w2, pe, mw)


def _one_query_attention(q8, tiles, k_new, v_new):
    qb = q8.astype(BF16)
    ss = []
    for k_t, _, mask in tiles:
        s = _dot(qb, k_t)
        ss.append(s if mask is None else jnp.where(mask, s, NEG))
    s_new = jnp.sum(q8 * k_new, axis=-1, keepdims=True)
    m = functools.reduce(jnp.maximum, [jnp.max(s, axis=-1, keepdims=True) for s in ss] + [s_new])
    es = [jnp.exp(s - m) for s in ss]
    e_new = jnp.exp(s_new - m)
    l = functools.reduce(jnp.add, [jnp.sum(e, axis=-1, keepdims=True) for e in es]) + e_new
    acc = functools.reduce(jnp.add, [_dot_nt(e.astype(BF16), t[1]) for e, t in zip(es, tiles)])
    return (acc + e_new * v_new) / l


def _selwin_sample_kernel(pt_ref, idx_ref, cache_ref, newsel_ref, win_ref, newwin_ref, q_ref,
                          osel_ref, owin_ref, buf_ref, sem, *, layer, n_cb):
    b = pl.program_id(0)
    per_page = PAGE // SEL_BLOCK

    def copy(g, k):
        blk = jnp.minimum(idx_ref[b, g * N_SEL + k], n_cb - 1)
        return pltpu.make_async_copy(cache_ref.at[layer, pt_ref[b, blk // per_page]], buf_ref.at[g, k], sem.at[0])

    for g in range(N_KV):
        for k in range(N_SEL):
            copy(g, k).start()
    q = q_ref[0]
    new_sel = newsel_ref[0]
    new_win = newwin_ref[0]
    win_k = win_ref[0, 0, :LANES, :].astype(BF16)
    win_v = win_ref[0, 0, LANES:, :].astype(BF16)
    for g in range(N_KV):
        owin_ref[0, g] = _one_query_attention(_q8(q, g), [(win_k, win_v, None)], new_win[:, :LANES], new_win[:, LANES:])
    for g in range(N_KV):
        for k in range(N_SEL):
            copy(g, k).wait()
    lane_blk = _iota((1, PAGE), 1) // SEL_BLOCK
    for g in range(N_KV):
        tiles = []
        for k in range(N_SEL):
            blk = idx_ref[b, g * N_SEL + k]
            mask = (lane_blk == blk % per_page) & (blk < n_cb)
            tiles.append((buf_ref[g, k, :LANES, :].astype(BF16), buf_ref[g, k, LANES:, :].astype(BF16), mask))
        osel_ref[0, g] = _one_query_attention(_q8(q, g), tiles, new_sel[:, :LANES], new_sel[:, LANES:])


def _selwin_sample(layer, pt, idx, cache_t, newsel, win_t, newwin, q, past):
    bd = pt.shape[0]
    wb = win_t.shape[-1]
    blk = lambda *s: pl.BlockSpec((1,) + s, lambda b_, *_: (b_,) + (0,) * len(s))
    gs = pltpu.PrefetchScalarGridSpec(
        num_scalar_prefetch=2, grid=(bd,),
        in_specs=[pl.BlockSpec(memory_space=pl.ANY), blk(1, KV_W),
                  pl.BlockSpec((1, 1, KV_W, wb), lambda b_, *_: (layer, b_, 0, 0)), blk(1, KV_W), blk(1, QPAD_W)],
        out_specs=[blk(N_KV, 8, LANES), blk(N_KV, 8, LANES)],
        scratch_shapes=[pltpu.VMEM((N_KV, N_SEL, KV_W, PAGE), F32), pltpu.SemaphoreType.DMA((1,))])
    return pl.pallas_call(
        functools.partial(_selwin_sample_kernel, layer=layer, n_cb=past // SEL_BLOCK),
        grid_spec=gs,
        out_shape=[jax.ShapeDtypeStruct((bd, N_KV, 8, LANES), F32)] * 2,
        compiler_params=_params(1),
        name="selwin_sample",
    )(pt, idx, cache_t, newsel, win_t, newwin, q)


MOBA_CHUNK_PAGES = 8


def _moba_sample_kernel(pt_ref, cache_ref, new_ref, q_ref, o_ref, x_ref, s_ref, sem, *, layer, n_pages, n_blk):
    slot = _paged_prologue(lambda b, p: cache_ref.at[layer, pt_ref[b, p]], lambda sl, p: x_ref.at[sl, p], sem, n_pages)
    q = q_ref[0]
    new = new_ref[0]
    per_blk = MOBA_BLOCK // PAGE
    lane_sq = _iota((LANES, LANES), 1)
    km_t = jnp.zeros((LANES, LANES), F32)
    for j in range(n_blk):
        blk = functools.reduce(jnp.add, [x_ref[slot, j * per_blk + t, :LANES, :] for t in range(per_blk)])
        km_t = jnp.where(lane_sq == j, jnp.sum(blk, axis=1, keepdims=True) * (1.0 / MOBA_BLOCK), km_t)
    jl = _iota((1, LANES), 1)
    row8 = _iota((8, LANES), 0)
    q8s, notsels = [], []
    for g in range(N_KV):
        q8 = _q8(q, g)
        gate = _dot(q8, km_t, HI)
        notsel = jnp.ones((8, LANES), F32)
        for r in range(REP):
            score = jnp.where(jl < n_blk, gate[r:r + 1, :], -jnp.inf)
            sel = (_rank_lanes(score) < MOBA_TOPK) & (jl < n_blk)
            notsel = jnp.where((row8 == r) & sel, 0.0, notsel)
        q8s.append(q8)
        notsels.append(notsel)
    cw = MOBA_CHUNK_PAGES * PAGE
    blk_per_chunk = cw // MOBA_BLOCK
    blk_of_lane = _iota((1, cw), 1) // MOBA_BLOCK
    n_chunks = n_pages // MOBA_CHUNK_PAGES
    for c in range(n_chunks):
        k_t = jnp.concatenate([x_ref[slot, c * MOBA_CHUNK_PAGES + t, :LANES, :] for t in range(MOBA_CHUNK_PAGES)],
                              axis=1).astype(BF16)
        for g in range(N_KV):
            off = jnp.zeros((8, cw), F32)
            for t in range(blk_per_chunk):
                j = c * blk_per_chunk + t
                off = jnp.where(blk_of_lane == t, notsels[g][:, j:j + 1], off)
            s_ref[g, :, c * cw:(c + 1) * cw] = jnp.where(off > 0.5, NEG, _dot(q8s[g].astype(BF16), k_t))
    ls, e_news = [], []
    for g in range(N_KV):
        s = s_ref[g]
        s_new = jnp.sum(q8s[g] * new[:, :LANES], axis=-1, keepdims=True)
        m = jnp.maximum(jnp.max(s, axis=-1, keepdims=True), s_new)
        e = jnp.exp(s - m)
        e_new = jnp.exp(s_new - m)
        ls.append(jnp.sum(e, axis=-1, keepdims=True) + e_new)
        e_news.append(e_new)
        s_ref[g] = e
    acc = [e_news[g] * new[:, LANES:] for g in range(N_KV)]
    for c in range(n_chunks):
        v_t = jnp.concatenate([x_ref[slot, c * MOBA_CHUNK_PAGES + t, LANES:, :] for t in range(MOBA_CHUNK_PAGES)],
                              axis=1).astype(BF16)
        for g in range(N_KV):
            acc[g] = acc[g] + _dot_nt(s_ref[g, :, c * cw:(c + 1) * cw].astype(BF16), v_t)
    for g in range(N_KV):
        o_ref[0, g] = acc[g] / ls[g]


def _moba_sample(layer, pt, cache_t, new, q, past):
    bd, n_pages = pt.shape
    n_blk = past // MOBA_BLOCK
    blk = lambda *s: pl.BlockSpec((1,) + s, lambda b_, pt_: (b_,) + (0,) * len(s))
    gs = pltpu.PrefetchScalarGridSpec(
        num_scalar_prefetch=1, grid=(bd,),
        in_specs=[pl.BlockSpec(memory_space=pl.ANY), blk(1, KV_W), blk(1, QPAD_W)],
        out_specs=blk(N_KV, 8, LANES),
        scratch_shapes=[pltpu.VMEM((2, n_pages, KV_W, PAGE), F32), pltpu.VMEM((N_KV, 8, past), F32),
                        pltpu.SemaphoreType.DMA((2,))])
    return pl.pallas_call(
        functools.partial(_moba_sample_kernel, layer=layer, n_pages=n_pages, n_blk=n_blk),
        grid_spec=gs,
        out_shape=jax.ShapeDtypeStruct((bd, N_KV, 8, LANES), F32),
        compiler_params=_params(1),
        name="moba_sample",
    )(pt, cache_t, new, q)


def _combine_sample_kernel(u_ref, st_ref, w_ref, b_ref, lg_ref, lb_ref, gates_ref, ocmp_ref, osel_ref, owin_ref,
                           omoba_ref, conv_ref, nsa_ref, moba_ref):
    y = u_ref[...] * w_ref[CONV_WIDTH - 1:CONV_WIDTH, :]
    for j in range(CONV_WIDTH - 1):
        y = y + st_ref[j] * w_ref[j:j + 1, :]
    conv_ref[...] = _conv_post(y, b_ref, lg_ref, lb_ref).astype(BF16)
    gates = gates_ref[...]
    w = N_HEADS * HEAD_DIM
    head_of_lane = _iota((1, w), 1) // HEAD_DIM
    acc = jnp.zeros(ocmp_ref.shape, F32)
    for br, ref in enumerate((ocmp_ref, osel_ref, owin_ref)):
        ge = jnp.zeros(ocmp_ref.shape, F32)
        for hh in range(N_HEADS):
            c = br * N_HEADS + hh
            ge = jnp.where(head_of_lane == hh, gates[:, c:c + 1], ge)
        acc = acc + ge * ref[...]
    nsa_ref[...] = acc.astype(BF16)
    moba_ref[...] = omoba_ref[...].astype(BF16)


def _combine_sample(u, st_t, w, b, lg, lb, gates, o_cmp, o_sel, o_win, o_moba):
    bd = u.shape[0]
    w_h = N_HEADS * HEAD_DIM
    return pl.pallas_call(
        _combine_sample_kernel,
        out_shape=[jax.ShapeDtypeStruct((bd, CONV_CH), BF16), jax.ShapeDtypeStruct((bd, w_h), BF16),
                   jax.ShapeDtypeStruct((bd, w_h), BF16)],
        name="combine_sample",
    )(u, st_t, w, b, lg, lb, gates, o_cmp, o_sel, o_win, o_moba)


def _rope_tables(pos):
    half = HEAD_DIM // 2
    inv = ROPE_THETA ** (-jnp.arange(half, dtype=F32) / half)
    ang = pos.astype(F32)[:, None] * inv[None, :]
    cos, sin = jnp.cos(ang), jnp.sin(ang)
    zero = jnp.zeros_like(sin)
    return (jnp.tile(cos, (1, 4)), jnp.tile(jnp.concatenate([-sin, zero], axis=1), (1, 2)),
            jnp.tile(jnp.concatenate([zero, sin], axis=1), (1, 2)))


def _pad_heads(w):
    d = w.shape[:-1]
    w = w.reshape(*d, N_HEADS, HEAD_DIM)
    return jnp.pad(w, [(0, 0)] * len(d) + [(0, 0), (0, LANES - HEAD_DIM)]).reshape(*d, QPAD_W)


def _relayout_w_in(w_in):
    c_q = 2 * CONV_CH
    c_kv = c_q + N_HEADS * HEAD_DIM
    c_g = c_kv + 6 * N_KV * HEAD_DIM
    c_qm = c_g + 3 * N_HEADS
    c_kvm = c_qm + N_HEADS * HEAD_DIM
    gates = jnp.pad(w_in[..., c_g:c_qm], ((0, 0), (0, 0), (0, LANES - 3 * N_HEADS)))
    return jnp.concatenate([w_in[..., :c_q], _pad_heads(w_in[..., c_q:c_kv]), w_in[..., c_kv:c_g],
                            _pad_heads(w_in[..., c_qm:c_kvm]), w_in[..., c_kvm:], gates], axis=-1).astype(BF16)


def _compress_weights(pe_k, pe_v, wk1, wk2, wv1, wv2):
    depth = wk1.shape[0]
    eye = jnp.eye(4, dtype=F32)
    w1 = jnp.stack([wk1, wk1, wv1, wv1], axis=1).reshape(depth, 4, 2, CMP_STRIDE, HEAD_DIM, CMP_HIDDEN)
    wbig = jnp.einsum('lsaidh,st->lisdath', w1, eye).reshape(depth, CHUNK_W, 8 * CMP_HIDDEN).astype(BF16)
    w2 = jnp.stack([wk2, wk2, wv2, wv2], axis=1)
    w2big = jnp.einsum('lshd,st->lshtd', w2, eye).reshape(depth, 4 * CMP_HIDDEN, KV_W).astype(BF16)
    pe = jnp.stack([pe_k, pe_k, pe_v, pe_v], axis=1).reshape(depth, 4, 2, CMP_STRIDE, HEAD_DIM)
    pe = pe.transpose(0, 2, 3, 1, 4).reshape(depth, 2, CHUNK_W)
    return wbig, w2big, jnp.pad(pe, ((0, 0), (0, 6), (0, 0)))


def _slc_matrix(n_c, n_sb):
    ratio = SEL_BLOCK // CMP_STRIDE
    c = np.arange(n_c)[:, None]
    j = np.arange(n_sb)[None, :]
    m = np.where(c == ratio * j, 1.0, 0.0) + np.where((c > ratio * j) & (c < ratio * (j + 1)), 2.0, 0.0) \
        + np.where(c == ratio * (j + 1), 1.0, 0.0)
    return m.astype(np.float32)


def _heads_from_q8(o):
    return jnp.concatenate([o[:, 0, :REP, :HEAD_DIM], o[:, 1, :REP, HEAD_DIM:]], axis=1).reshape(o.shape[0], -1)


def kernel(x_prompt, x_sample, cache_nsa_cmp, cache_nsa_sel, cache_moba, state_nsa_win, state_conv, page_table,
           g_mix, w_in, conv_w, conv_b, conv_ln_g, conv_ln_b, cmp_pe_k, cmp_pe_v, cmp_wk1, cmp_wk2, cmp_wv1, cmp_wv2,
           w_out, g_ffn, w_gate_up, w_down, g_final):
    nb, seq, _ = x_prompt.shape
    bd, dec_seq, _ = x_sample.shape
    depth = g_mix.shape[0]
    n_pages = page_table.shape[1]
    past = n_pages * PAGE
    assert dec_seq == 1 and seq % 512 == 0 and seq // SEL_BLOCK <= HEAD_DIM and 3 * (seq // MOBA_BLOCK) <= HEAD_DIM
    assert past % MOBA_BLOCK == 0 and past // MOBA_BLOCK <= LANES and bd % 8 == 0 and n_pages % MOBA_CHUNK_PAGES == 0
    n_p = nb * seq
    tm = 512

    w_in_r = _relayout_w_in(w_in)
    wbig, w2big, pe_rows = _compress_weights(cmp_pe_k, cmp_pe_v, cmp_wk1, cmp_wk2, cmp_wv1, cmp_wv2)
    w_out_b, w_gu_b, w_down_b = w_out.astype(BF16), w_gate_up.astype(BF16), w_down.astype(BF16)
    conv_w_p = jnp.pad(conv_w, ((0, 0), (0, CONV_HALO - CONV_WIDTH), (0, 0)))
    tab_p = _rope_tables(jnp.arange(seq))
    tab_s = _rope_tables(jnp.full((bd,), past))
    n_c = seq // CMP_STRIDE
    mwt_p = jnp.asarray(_slc_matrix(n_c, seq // SEL_BLOCK).T)
    n_sb_s = past // SEL_BLOCK + 1
    mw_s = jnp.asarray(np.pad(_slc_matrix(past // CMP_STRIDE, n_sb_s), ((0, 0), (0, -n_sb_s % LANES))))

    feat_major = lambda a: a.transpose(0, 1, 3, 4, 5, 2).reshape(a.shape[0], a.shape[1], KV_W, a.shape[2])
    cmp_t, sel_t, moba_t, win_t = (feat_major(a) for a in (cache_nsa_cmp, cache_nsa_sel, cache_moba, state_nsa_win))

    xp = x_prompt.reshape(n_p, D_MODEL)
    xs = x_sample.reshape(bd, D_MODEL)
    new_p = [[], [], [], [], []]
    new_s = [[], [], [], [], []]
    r2 = lambda a: a.reshape(1, -1)
    for l in range(depth):
        (u, q, cmp_rows, sel_rows, win_rows, moba_rows, qm, gates,
         selk, selv, wink, winv, mobak, mobav) = _proj(xp, r2(g_mix[l]), w_in_r[l], *tab_p, seq, tm)
        conv_o = _conv(u, conv_w_p[l], r2(conv_b[l]), r2(conv_ln_g[l]), r2(conv_ln_b[l]), nb, seq, 512)
        comp = _compress_prompt(cmp_rows, wbig[l], w2big[l], pe_rows[l], nb, seq)
        nsa_o = _nsa_prompt(q, gates, comp, selk, selv, wink, winv, mwt_p, nb, seq)
        moba_o = _moba_prompt(qm, moba_rows, mobak, mobav, nb, seq)
        xp = _ffn(xp, conv_o, nsa_o, moba_o, w_out_b[l], r2(g_ffn[l]), w_gu_b[l], w_down_b[l], tm)
        kv6 = lambda a, n_: a.reshape(n_, -1, 2, N_KV, HEAD_DIM)
        new_p[0].append(kv6(cmp_rows, nb))
        new_p[1].append(kv6(sel_rows, nb))
        new_p[2].append(kv6(moba_rows, nb))
        new_p[3].append(kv6(win_rows, nb)[:, -min(WINDOW, seq):])
        new_p[4].append(u.reshape(nb, seq, CONV_CH)[:, -(CONV_WIDTH - 1):])

        (u, q, cmp_rows, sel_rows, win_rows, moba_rows, qm, gates, *_) = _proj(xs, r2(g_mix[l]), w_in_r[l], *tab_s, bd, bd)
        b3 = lambda a: a.reshape(bd, 1, -1)
        o_cmp, idx = _cmp_sample(l, page_table, cmp_t, b3(cmp_rows), b3(q), wbig[l], w2big[l], pe_rows[l],
                                 mw_s, past)
        o_sel, o_win = _selwin_sample(l, page_table, idx[:, :, :, 0].reshape(bd, N_KV * N_SEL), sel_t,
                                      b3(sel_rows), win_t, b3(win_rows), b3(q), past)
        o_moba = _moba_sample(l, page_table, moba_t, b3(moba_rows), b3(qm), past)
        conv_o, nsa_o, moba_o = _combine_sample(
            u, state_conv[l].transpose(1, 0, 2), conv_w_p[l], r2(conv_b[l]), r2(conv_ln_g[l]), r2(conv_ln_b[l]), gates,
            _heads_from_q8(o_cmp), _heads_from_q8(o_sel), _heads_from_q8(o_win), _heads_from_q8(o_moba))
        xs = _ffn(xs, conv_o, nsa_o, moba_o, w_out_b[l], r2(g_ffn[l]), w_gu_b[l], w_down_b[l], bd)
        new_s[0].append(kv6(cmp_rows, bd))
        new_s[1].append(kv6(sel_rows, bd))
        new_s[2].append(kv6(moba_rows, bd))
        new_s[3].append(jnp.concatenate([state_nsa_win[l], kv6(win_rows, bd)], axis=1)[:, -state_nsa_win.shape[2]:])
        new_s[4].append(jnp.concatenate([state_conv[l], u[:, None, :]], axis=1)[:, -(CONV_WIDTH - 1):])

    y_prompt = _final_norm(xp, r2(g_final), tm).reshape(nb, seq, D_MODEL)
    y_sample = _final_norm(xs, r2(g_final), bd).reshape(bd, 1, D_MODEL)
    return (y_prompt, y_sample,
            jnp.stack(new_p[0]), jnp.stack(new_s[0]),
            jnp.stack(new_p[1]), jnp.stack(new_s[1]),
            jnp.stack(new_p[2]), jnp.stack(new_s[2]),
            jnp.stack(new_p[3]), jnp.stack(new_s[3]),
            jnp.stack(new_p[4]), jnp.stack(new_s[4]))
```

```python
import functools

import numpy as np
import jax
import jax.numpy as jnp
from jax import lax
from jax.experimental import pallas as pl
from jax.experimental.pallas import tpu as pltpu

F32 = jnp.float32
BF16 = jnp.bfloat16
I32 = jnp.int32

D_MODEL = 1024
HEAD_DIM = 64
LANES = 128
CONV_CH = 256
N_HEADS = 6
N_KV = 2
REP = N_HEADS // N_KV
CONV_WIDTH = 31
CMP_STRIDE = 16
CMP_HIDDEN = 128
SEL_BLOCK = 64
N_SEL = 16
WINDOW = 512
BAND = 128
MOBA_BLOCK = 256
MOBA_TOPK = 3
PAGE = 128
FFN_HIDDEN = 2816
ROPE_THETA = 10000.0
EPS = 1e-6
NEG = -1e30
BIG = 1e30
ATTN_SCALE = HEAD_DIM ** -0.5
HI = lax.Precision.HIGHEST

KV_W = 2 * N_KV * HEAD_DIM
QPAD_W = N_HEADS * LANES
CHUNK_W = CMP_STRIDE * KV_W

OFF_U, OFF_Q, OFF_KV, OFF_QM, OFF_KVM, OFF_G, W_IN_COLS = 0, 512, 1280, 2048, 2816, 3072, 3200

VMEM_LIMIT = 56 * 1024 * 1024


def _params(n_axes, limit=VMEM_LIMIT):
    return pltpu.CompilerParams(dimension_semantics=("arbitrary",) * n_axes, vmem_limit_bytes=limit)


def _const_spec(shape):
    nd = len(shape)
    return pl.BlockSpec(shape, lambda *_: (0,) * nd, pipeline_mode=pl.Buffered(1))


def _dot(a, b, precision=None):
    return jnp.dot(a, b, preferred_element_type=F32, precision=precision)


def _dot_nt(a, b, precision=None):
    return lax.dot_general(a, b, (((1,), (1,)), ((), ())), preferred_element_type=F32, precision=precision)


def _iota(shape, dim):
    return lax.broadcasted_iota(I32, shape, dim)


def _rms(x, g):
    return x * lax.rsqrt(jnp.mean(x * x, axis=-1, keepdims=True) + EPS) * g


def _compact_heads(a):
    lo = _iota(a[0].shape, 1) < HEAD_DIM
    return jnp.concatenate([
        jnp.where(lo, a[0], pltpu.roll(a[1], HEAD_DIM, 1)),
        jnp.where(lo, a[2], a[3]),
        jnp.where(lo, pltpu.roll(a[4], HEAD_DIM, 1), a[5])], axis=1)


def _proj_kernel(x_ref, g_ref, w_ref, cos_ref, sa_ref, sb_ref,
                 u_ref, q_ref, cmp_ref, sel_ref, win_ref, moba_ref, qm_ref, gates_ref,
                 selk_ref, selv_ref, wink_ref, winv_ref, mobak_ref, mobav_ref, *, seq, tm):
    i = pl.program_id(0)
    h = _rms(x_ref[...], g_ref[...]).astype(BF16)
    cos, sa, sb = cos_ref[...], sa_ref[...], sb_ref[...]

    def seg(off, width):
        return _dot(h, w_ref[:, off:off + width])

    def rope(z):
        return z * cos + pltpu.roll(z, LANES - 32, 1) * sa + pltpu.roll(z, 32, 1) * sb

    zu = seg(OFF_U, 2 * CONV_CH)
    u_ref[...] = zu[:, :CONV_CH] * jax.nn.sigmoid(zu[:, CONV_CH:])

    for off, ref in ((OFF_Q, q_ref), (OFF_QM, qm_ref)):
        z = seg(off, QPAD_W)
        for hh in range(N_HEADS):
            ref[:, hh * LANES:(hh + 1) * LANES] = rope(z[:, hh * LANES:(hh + 1) * LANES]) * ATTN_SCALE

    t = (i * tm) % seq + _iota((tm, LANES), 0)
    lane = _iota((tm, LANES), 1)
    lo = lane < HEAD_DIM
    oh_sel = jnp.where((lane >= HEAD_DIM) & ((t >> 6) == lane - HEAD_DIM), NEG, 0.0)
    oh_moba = jnp.where((lane >= HEAD_DIM) & (lane < HEAD_DIM + REP * MOBA_LANES)
                        & ((t >> 8) == ((lane - HEAD_DIM) & (MOBA_LANES - 1))), NEG, 0.0)

    def rows(off, ref):
        z = seg(off, KV_W)
        k = rope(z[:, :LANES])
        v = z[:, LANES:]
        ref[:, :LANES] = k
        ref[:, LANES:] = v
        return k, v

    rows(OFF_KV, cmp_ref)
    k, v = rows(OFF_KV + KV_W, sel_ref)
    selk_ref[:, :LANES] = jnp.where(lo, k, oh_sel).astype(BF16)
    selk_ref[:, LANES:] = jnp.where(lo, pltpu.roll(k, HEAD_DIM, 1), oh_sel).astype(BF16)
    selv_ref[...] = v.astype(BF16)
    k, v = rows(OFF_KV + 2 * KV_W, win_ref)
    wink_ref[:, :LANES] = k.astype(BF16)
    wink_ref[:, LANES:] = pltpu.roll(k, HEAD_DIM, 1).astype(BF16)
    winv_ref[...] = v.astype(BF16)
    k, v = rows(OFF_KVM, moba_ref)
    mobak_ref[:, :LANES] = jnp.where(lo, k, oh_moba).astype(BF16)
    mobak_ref[:, LANES:] = jnp.where(lo, pltpu.roll(k, HEAD_DIM, 1), oh_moba).astype(BF16)
    mobav_ref[...] = v.astype(BF16)

    gates_ref[...] = jax.nn.sigmoid(seg(OFF_G, LANES))


def _proj(x, g, w, cos, sa, sb, seq, tm):
    n = x.shape[0]
    nt = cos.shape[0] // tm
    row = lambda wd: pl.BlockSpec((tm, wd), lambda i: (i, 0))
    tab = pl.BlockSpec((tm, LANES), lambda i: (i % nt, 0))
    f32_w = (CONV_CH, QPAD_W, KV_W, KV_W, KV_W, KV_W, QPAD_W, LANES)
    bf_w = (2 * LANES, LANES, 2 * LANES, LANES, 2 * LANES, LANES)
    return pl.pallas_call(
        functools.partial(_proj_kernel, seq=seq, tm=tm),
        grid=(n // tm,),
        in_specs=[row(D_MODEL), _const_spec((1, D_MODEL)), _const_spec((D_MODEL, W_IN_COLS)), tab, tab, tab],
        out_specs=[row(wd) for wd in f32_w + bf_w],
        out_shape=[jax.ShapeDtypeStruct((n, wd), F32) for wd in f32_w]
        + [jax.ShapeDtypeStruct((n, wd), BF16) for wd in bf_w],
        compiler_params=_params(1),
        name="proj",
    )(x, g, w, cos, sa, sb)


CONV_HALO = 32


def _conv_post(y, b_ref, lg_ref, lb_ref):
    y = y + b_ref[...]
    mu = jnp.mean(y, axis=-1, keepdims=True)
    var = jnp.mean(jnp.square(y - mu), axis=-1, keepdims=True)
    y = (y - mu) * lax.rsqrt(var + EPS) * lg_ref[...] + lb_ref[...]
    return y * jax.nn.sigmoid(y)


def _conv_kernel(u_ref, prev_ref, w_ref, b_ref, lg_ref, lb_ref, o_ref, ext_ref, *, tq):
    i = pl.program_id(1)
    ext_ref[:CONV_HALO, :] = jnp.where(i > 0, prev_ref[...], 0.0)
    ext_ref[CONV_HALO:, :] = u_ref[...]
    off = CONV_HALO - (CONV_WIDTH - 1)
    y = jnp.zeros((tq, CONV_CH), F32)
    for j in range(CONV_WIDTH):
        y = y + ext_ref[pl.ds(off + j, tq), :] * w_ref[j:j + 1, :]
    o_ref[...] = _conv_post(y, b_ref, lg_ref, lb_ref).astype(BF16)


def _conv(u, w, b, lg, lb, nb, seq, tq):
    n = u.shape[0]
    nq = seq // tq
    per = tq // CONV_HALO
    return pl.pallas_call(
        functools.partial(_conv_kernel, tq=tq),
        grid=(nb, nq),
        in_specs=[pl.BlockSpec((tq, CONV_CH), lambda b_, i: (b_ * nq + i, 0)),
                  pl.BlockSpec((CONV_HALO, CONV_CH), lambda b_, i: (jnp.maximum((b_ * nq + i) * per - 1, 0), 0)),
                  _const_spec((CONV_HALO, CONV_CH)), _const_spec((1, CONV_CH)), _const_spec((1, CONV_CH)),
                  _const_spec((1, CONV_CH))],
        out_specs=pl.BlockSpec((tq, CONV_CH), lambda b_, i: (b_ * nq + i, 0)),
        out_shape=jax.ShapeDtypeStruct((n, CONV_CH), BF16),
        scratch_shapes=[pltpu.VMEM((tq + CONV_HALO, CONV_CH), F32)],
        compiler_params=_params(2),
        name="conv",
    )(u, u, w, b, lg, lb)


def _gelu_tanh(x):
    return x * (0.5 * (1.0 + jnp.tanh(0.7978845608028654 * (x + 0.044715 * (x * x * x)))))


CMP_TAIL = 16
CMP_PAIRS_PER_TRIP = 4


def _compress(x_bf, n_tok, wbig_ref, w2_ref, pe_ref, ab_ref):
    half = 4 * CMP_HIDDEN
    ab_ref[...] = _dot(x_bf, wbig_ref[...])
    pw = _dot(pe_ref[...].astype(BF16), wbig_ref[...])
    pe_all = pw[0:1, :half] + pw[1:2, half:]
    hid = ab_ref[pl.ds(0, n_tok), :half] + ab_ref[pl.ds(1, n_tok), half:] + pe_all
    return _dot(_gelu_tanh(hid).astype(BF16), w2_ref[...])


def _compress_kernel(x_ref, wbig_ref, w2_ref, pe_ref, o_ref, x_scr, ab_ref, *, n_tok):
    x_scr[pl.ds(0, n_tok), :] = x_ref[0].astype(BF16)
    x_scr[pl.ds(n_tok, CMP_TAIL), :] = jnp.zeros((CMP_TAIL, CHUNK_W), BF16)
    o_ref[0] = _compress(x_scr[...], n_tok, wbig_ref, w2_ref, pe_ref, ab_ref)


def _compress_prompt(rows, wbig, w2, pe, nb, seq):
    n_tok = seq // CMP_STRIDE
    x = rows.reshape(nb, n_tok, CHUNK_W)
    return pl.pallas_call(
        functools.partial(_compress_kernel, n_tok=n_tok),
        grid=(nb,),
        in_specs=[pl.BlockSpec((1, n_tok, CHUNK_W), lambda b_: (b_, 0, 0)),
                  _const_spec((CHUNK_W, 8 * CMP_HIDDEN)), _const_spec((4 * CMP_HIDDEN, KV_W)),
                  _const_spec((8, CHUNK_W))],
        out_specs=pl.BlockSpec((1, n_tok, KV_W), lambda b_: (b_, 0, 0)),
        out_shape=jax.ShapeDtypeStruct((nb, n_tok, KV_W), F32),
        scratch_shapes=[pltpu.VMEM((n_tok + CMP_TAIL, CHUNK_W), BF16),
                        pltpu.VMEM((n_tok + CMP_TAIL, 8 * CMP_HIDDEN), F32)],
        compiler_params=_params(1),
        name="compress_prompt",
    )(x, wbig, w2, pe)


def _softmax_rows(s, mask):
    s = jnp.where(mask, s, NEG)
    m = jnp.max(s, axis=-1, keepdims=True)
    e = jnp.where(mask, jnp.exp(s - m), 0.0)
    l = jnp.sum(e, axis=-1, keepdims=True)
    return e / jnp.where(l > 0.0, l, 1.0)


def _rank_rows(score, n):
    jj = _iota(score.shape, 0)
    rank = jnp.zeros(score.shape, I32)
    for ii in range(n):
        row = score[ii:ii + 1, :]
        rank = rank + jnp.where((row > score) | ((row == score) & (ii < jj)), 1, 0)
    return rank


def _flash_step(carry, s, v):
    m, l, acc = carry
    m_new = jnp.maximum(m, jnp.max(s, axis=-1, keepdims=True))
    a = jnp.exp(m - m_new)
    p = jnp.exp(s - m_new)
    return m_new, a * l + jnp.sum(p, axis=-1, keepdims=True), a * acc + _dot(p.astype(BF16), v)


def _flash_init(rows):
    return jnp.full((rows, 1), NEG, F32), jnp.zeros((rows, 1), F32), jnp.zeros((rows, LANES), F32)


NSA_TQ = 256
SEL_TK = 512


def _nsa_kernel(q_ref, gates_ref, comp_ref, selk_ref, selv_ref, wink_ref, winv_ref, mwt_ref, o_ref, *, tq):
    i = pl.program_id(1)
    t0 = i * tq
    q = q_ref[...]
    comp = comp_ref[0]
    n_c = comp.shape[0]
    kc = comp[:, :LANES]
    kc_g = (kc, pltpu.roll(kc, HEAD_DIM, 1))
    vc = comp[:, LANES:].astype(BF16)
    pos_col = t0 + _iota((tq, 1), 0)
    cmask = (_iota((1, n_c), 1) * CMP_STRIDE + (2 * CMP_STRIDE - 1)) <= pos_col
    n_sb = mwt_ref.shape[0]

    def head(hh):
        return q[:, hh * LANES:(hh + 1) * LANES]

    o_cmp, ns = [], []
    for g in range(N_KV):
        imp = jnp.zeros((tq, n_c), F32)
        for r in range(REP):
            p = _softmax_rows(_dot_nt(head(g * REP + r), kc_g[g], HI), cmask)
            imp = imp + p
            o_cmp.append(_dot(p.astype(BF16), vc))
        pslc_t = _dot_nt(mwt_ref[...], imp, HI)
        jj = _iota((n_sb, tq), 0)
        cur = (t0 + _iota((n_sb, tq), 1)) >> 6
        forced = (jj == 0) | (jj == cur) | (jj == cur - 1)
        score = jnp.where(forced, BIG, jnp.where(jj <= cur, pslc_t, NEG))
        notsel = jnp.where(_rank_rows(score, n_sb) < N_SEL, 0.0, 1.0)
        parts = [jnp.zeros((HEAD_DIM, tq), F32), notsel]
        if n_sb < HEAD_DIM:
            parts.append(jnp.zeros((HEAD_DIM - n_sb, tq), F32))
        ns.append(jnp.concatenate(parts, axis=0).T.astype(BF16))

    row_pos = t0 + (_iota((REP * tq, 1), 0) % tq)

    qas = [jnp.concatenate([head(g * REP + r).astype(BF16) + ns[g] for r in range(REP)], axis=0) for g in range(N_KV)]

    def tile(kt):
        k0 = pl.multiple_of(kt * SEL_TK, SEL_TK)
        v = selv_ref[pl.ds(k0, SEL_TK), :]
        return [_dot_nt(qas[g], selk_ref[pl.ds(k0, SEL_TK), g * LANES:(g + 1) * LANES]) for g in range(N_KV)], v

    def body(kt, carry):
        ss, v = tile(kt)
        return tuple(_flash_step(carry[g], ss[g], v) for g in range(N_KV))

    last = t0 // SEL_TK
    carry = lax.fori_loop(0, last, body, tuple(_flash_init(REP * tq) for _ in range(N_KV)))
    ss, v = tile(last)
    kpos = last * SEL_TK + _iota((1, SEL_TK), 1)
    o_sel = []
    for g in range(N_KV):
        m, l, acc = _flash_step(carry[g], jnp.where(kpos <= row_pos, ss[g], NEG), v)
        o = acc / l
        o_sel += [o[r * tq:(r + 1) * tq] for r in range(REP)]

    o_win = []
    n_w = (WINDOW + tq) // BAND
    cc = _iota((1, BAND), 1)
    rr = _iota((REP * tq, 1), 0) % tq
    for g in range(N_KV):
        qw = jnp.concatenate([head(g * REP + r).astype(BF16) for r in range(REP)], axis=0)
        ss, vs = [], []
        for j in range(n_w):
            k0 = t0 - WINDOW + j * BAND
            ok = k0 >= 0
            k0c = pl.multiple_of(jnp.maximum(k0, 0), BAND)
            s = _dot_nt(qw, wink_ref[pl.ds(k0c, BAND), g * LANES:(g + 1) * LANES])
            if BAND * j < tq:
                ok = ok & (cc >= rr - BAND * j)
            if WINDOW - BAND * j < BAND:
                ok = ok & (cc <= rr + (WINDOW - BAND * j))
            ss.append(jnp.where(ok, s, NEG))
            vs.append(winv_ref[pl.ds(k0c, BAND), :])
        m = functools.reduce(jnp.maximum, [jnp.max(s, axis=-1, keepdims=True) for s in ss])
        es = [jnp.exp(s - m) for s in ss]
        l = functools.reduce(jnp.add, [jnp.sum(e, axis=-1, keepdims=True) for e in es])
        acc = functools.reduce(jnp.add, [_dot(e.astype(BF16), v) for e, v in zip(es, vs)])
        o = acc / l
        o_win += [o[r * tq:(r + 1) * tq] for r in range(REP)]

    gates = gates_ref[...]
    mixed = []
    for hh in range(N_HEADS):
        gc = [gates[:, br * N_HEADS + hh:br * N_HEADS + hh + 1] for br in range(3)]
        mixed.append(gc[0] * o_cmp[hh] + gc[1] * o_sel[hh] + gc[2] * o_win[hh])
    o_ref[...] = _compact_heads(mixed).astype(BF16)


def _nsa_prompt(q, gates, comp, selk, selv, wink, winv, mwt, nb, seq):
    tq = NSA_TQ
    nq = seq // tq
    n = q.shape[0]
    n_c = comp.shape[1]
    row = lambda wd: pl.BlockSpec((tq, wd), lambda b_, i: (b_ * nq + i, 0))
    full = lambda wd: pl.BlockSpec((seq, wd), lambda b_, i: (b_, 0))
    return pl.pallas_call(
        functools.partial(_nsa_kernel, tq=tq),
        grid=(nb, nq),
        in_specs=[row(QPAD_W), row(LANES), pl.BlockSpec((1, n_c, KV_W), lambda b_, i: (b_, 0, 0)),
                  full(2 * LANES), full(LANES), full(2 * LANES), full(LANES), _const_spec(mwt.shape)],
        out_specs=row(N_HEADS * HEAD_DIM),
        out_shape=jax.ShapeDtypeStruct((n, N_HEADS * HEAD_DIM), BF16),
        compiler_params=_params(2),
        name="nsa_prompt",
    )(q, gates, comp, selk, selv, wink, winv, mwt)


MOBA_LANES = 16


def _moba_kernel(q_ref, rows_ref, k_ref, v_ref, o_ref, kmean_ref, *, tq, n_blk):
    i = pl.program_id(1)

    @pl.when(i == 0)
    def _():
        for j in range(n_blk):
            kmean_ref[j:j + 1, :] = jnp.sum(rows_ref[j * MOBA_BLOCK:(j + 1) * MOBA_BLOCK, :LANES],
                                            axis=0, keepdims=True) * (1.0 / MOBA_BLOCK)

    q = q_ref[...]
    km = kmean_ref[...]
    km_g = (km, pltpu.roll(km, HEAD_DIM, 1))
    jj = _iota((n_blk, tq), 0)
    lane = _iota((tq, LANES), 1)
    cc = _iota((1, MOBA_BLOCK), 1)
    rr = _iota((REP * tq, 1), 0) % tq

    def head(hh):
        return q[:, hh * LANES:(hh + 1) * LANES]

    qas, qos = [], []
    for g in range(N_KV):
        ns_rows = [jnp.zeros((HEAD_DIM, tq), F32)]
        for r in range(REP):
            gate = jnp.where(jj < i, _dot_nt(km_g[g], head(g * REP + r), HI), NEG)
            sel = (_rank_rows(gate, n_blk) < MOBA_TOPK) & (jj < i)
            ns_rows.append(jnp.where(sel, 0.0, 1.0))
            if n_blk < MOBA_LANES:
                ns_rows.append(jnp.zeros((MOBA_LANES - n_blk, tq), F32))
        ns_rows.append(jnp.zeros((HEAD_DIM - REP * MOBA_LANES, tq), F32))
        ns = jnp.concatenate(ns_rows, axis=0).T
        qa, qo = [], []
        for r in range(REP):
            mine = (lane >= HEAD_DIM + r * MOBA_LANES) & (lane < HEAD_DIM + (r + 1) * MOBA_LANES)
            qh = head(g * REP + r).astype(BF16)
            qo.append(qh)
            qa.append(qh + jnp.where(mine, ns, 0.0).astype(BF16))
        qas.append(jnp.concatenate(qa, axis=0))
        qos.append(jnp.concatenate(qo, axis=0))

    def scores(k0, width, qs):
        return [_dot_nt(qs[g], k_ref[pl.ds(k0, width), g * LANES:(g + 1) * LANES]) for g in range(N_KV)]

    pair = 2 * MOBA_BLOCK

    def body(kt, carry):
        k0 = pl.multiple_of(kt * pair, pair)
        ss = scores(k0, pair, qas)
        v = v_ref[pl.ds(k0, pair), :]
        return tuple(_flash_step(carry[g], ss[g], v) for g in range(N_KV))

    carry = lax.fori_loop(0, i // 2, body, tuple(_flash_init(REP * tq) for _ in range(N_KV)))
    odd = (i % 2) == 1
    kp = pl.multiple_of(jnp.maximum(i - 1, 0) * MOBA_BLOCK, MOBA_BLOCK)
    ko = pl.multiple_of(i * MOBA_BLOCK, MOBA_BLOCK)
    s_prev = scores(kp, MOBA_BLOCK, qas)
    s_own = scores(ko, MOBA_BLOCK, qos)
    v = jnp.concatenate([v_ref[pl.ds(kp, MOBA_BLOCK), :], v_ref[pl.ds(ko, MOBA_BLOCK), :]], axis=0)
    outs = []
    for g in range(N_KV):
        s = jnp.concatenate([jnp.where(odd, s_prev[g], NEG), jnp.where(cc <= rr, s_own[g], NEG)], axis=1)
        m, l, acc = _flash_step(carry[g], s, v)
        o = acc / l
        outs += [o[r * tq:(r + 1) * tq] for r in range(REP)]
    o_ref[...] = _compact_heads(outs).astype(BF16)


def _moba_prompt(qm, rows, mobak, mobav, nb, seq):
    tq = MOBA_BLOCK
    nq = seq // tq
    n = qm.shape[0]
    full = lambda wd: pl.BlockSpec((seq, wd), lambda b_, i: (b_, 0))
    return pl.pallas_call(
        functools.partial(_moba_kernel, tq=tq, n_blk=nq),
        grid=(nb, nq),
        in_specs=[pl.BlockSpec((tq, QPAD_W), lambda b_, i: (b_ * nq + i, 0)),
                  full(KV_W), full(2 * LANES), full(LANES)],
        out_specs=pl.BlockSpec((tq, N_HEADS * HEAD_DIM), lambda b_, i: (b_ * nq + i, 0)),
        out_shape=jax.ShapeDtypeStruct((n, N_HEADS * HEAD_DIM), BF16),
        scratch_shapes=[pltpu.VMEM((nq, LANES), F32)],
        compiler_params=_params(2),
        name="moba_prompt",
    )(qm, rows, mobak, mobav)


FFN_CHUNK = 1408


def _ffn_kernel(x_ref, conv_ref, nsa_ref, moba_ref, wo_ref, g_ref, wgu_ref, wd_ref, o_ref):
    mix = jnp.concatenate([conv_ref[...], nsa_ref[...], moba_ref[...]], axis=1)
    x1 = x_ref[...] + _dot(mix, wo_ref[...])
    h2 = _rms(x1, g_ref[...]).astype(BF16)
    acc = x1
    for c0 in range(0, FFN_HIDDEN, FFN_CHUNK):
        gt = _dot(h2, wgu_ref[:, c0:c0 + FFN_CHUNK])
        up = _dot(h2, wgu_ref[:, FFN_HIDDEN + c0:FFN_HIDDEN + c0 + FFN_CHUNK])
        acc = acc + _dot((gt * jax.nn.sigmoid(gt) * up).astype(BF16), wd_ref[c0:c0 + FFN_CHUNK, :])
    o_ref[...] = acc


def _ffn(x, conv_o, nsa_o, moba_o, wo, g, wgu, wd, tm):
    n = x.shape[0]
    row = lambda wd_: pl.BlockSpec((tm, wd_), lambda i: (i, 0))
    return pl.pallas_call(
        _ffn_kernel,
        grid=(n // tm,),
        in_specs=[row(D_MODEL), row(CONV_CH), row(N_HEADS * HEAD_DIM), row(N_HEADS * HEAD_DIM),
                  _const_spec((D_MODEL, D_MODEL)), _const_spec((1, D_MODEL)),
                  _const_spec((D_MODEL, 2 * FFN_HIDDEN)), _const_spec((FFN_HIDDEN, D_MODEL))],
        out_specs=row(D_MODEL),
        out_shape=jax.ShapeDtypeStruct((n, D_MODEL), F32),
        compiler_params=_params(1),
        name="mix_ffn",
    )(x, conv_o, nsa_o, moba_o, wo, g, wgu, wd)


def _norm_kernel(x_ref, g_ref, o_ref):
    o_ref[...] = _rms(x_ref[...], g_ref[...])


def _final_norm(x, g, tm):
    n = x.shape[0]
    return pl.pallas_call(
        _norm_kernel,
        grid=(n // tm,),
        in_specs=[pl.BlockSpec((tm, D_MODEL), lambda i: (i, 0)), _const_spec((1, D_MODEL))],
        out_specs=pl.BlockSpec((tm, D_MODEL), lambda i: (i, 0)),
        out_shape=jax.ShapeDtypeStruct((n, D_MODEL), F32),
        compiler_params=_params(1),
        name="final_norm",
    )(x, g)


def _q8(q, g):
    row = _iota((8, LANES), 0)
    out = jnp.zeros((8, LANES), F32)
    for r in range(REP):
        hh = g * REP + r
        out = jnp.where(row == r, q[:, hh * LANES:(hh + 1) * LANES], out)
    return pltpu.roll(out, HEAD_DIM, 1) if g == 1 else out


def _gather_pages(page_src, page_dst, sem, n_pages, start):
    def body(p, _):
        cp = pltpu.make_async_copy(page_src(p), page_dst(p), sem)
        if start:
            cp.start()
        else:
            cp.wait()
        return 0
    lax.fori_loop(0, n_pages, body, 0)


def _paged_prologue(src_of, dst_of, sem, n_pages):
    b = pl.program_id(0)
    slot = b % 2

    def run(bb, sl, start):
        _gather_pages(lambda p: src_of(bb, p), lambda p: dst_of(sl, p), sem.at[sl], n_pages, start)

    @pl.when(b == 0)
    def _():
        run(0, 0, True)

    run(b, slot, False)

    @pl.when(b + 1 < pl.num_programs(0))
    def _():
        run(b + 1, 1 - slot, True)

    return slot


def _rank_lanes(score):
    n = score.shape[1]
    s_row = jnp.broadcast_to(score, (n, n))
    s_col = s_row.T
    ii = _iota((n, n), 0)
    jj = _iota((n, n), 1)
    beats = (s_col > s_row) | ((s_col == s_row) & (ii < jj))
    return jnp.sum(jnp.where(beats, 1, 0), axis=0, keepdims=True)


def _cmp_sample_kernel(pt_ref, cache_ref, new_ref, q_ref, wbig_ref, w2_ref, pe_ref, mw_ref, perm_ref,
                       o_ref, idx_ref, x_ref, xb_ref, ab_ref, sem, *, layer, n_pages, past):
    slot = _paged_prologue(lambda b, p: cache_ref.at[layer, pt_ref[b, p]], lambda sl, p: x_ref.at[sl, p], sem, n_pages)
    per_page = PAGE // CMP_STRIDE

    def pair_to_chunks(pp):
        r0 = pl.multiple_of(pp * 2 * per_page, 2 * per_page)
        tr = []
        for t in range(2):
            xp = _dot(x_ref[slot, 2 * pp + t].astype(BF16), perm_ref[...])
            tr.append([xp[hv * LANES:(hv + 1) * LANES, :].T for hv in range(2)])
        for i in range(CMP_STRIDE):
            for hv in range(2):
                piece = jnp.concatenate([tr[t][hv][i * per_page:(i + 1) * per_page, :] for t in range(2)], axis=0)
                c0 = i * KV_W + hv * LANES
                xb_ref[pl.ds(r0, 2 * per_page), c0:c0 + LANES] = piece.astype(BF16)

    def to_chunks(t, _):
        for u in range(CMP_PAIRS_PER_TRIP):
            pair_to_chunks(t * CMP_PAIRS_PER_TRIP + u)
        return 0

    lax.fori_loop(0, n_pages // (2 * CMP_PAIRS_PER_TRIP), to_chunks, 0)
    n_tok = n_pages * per_page
    new_chunk = jnp.concatenate([new_ref[0], jnp.zeros((1, CHUNK_W - KV_W), F32)], axis=1)
    xb_ref[pl.ds(n_tok, CMP_TAIL), :] = jnp.where(_iota((CMP_TAIL, 1), 0) == 0, new_chunk, 0.0).astype(BF16)
    comp = _compress(xb_ref[...], n_tok, wbig_ref, w2_ref, pe_ref, ab_ref)
    kc = comp[:, :LANES]
    vc = comp[:, LANES:].astype(BF16)
    q = q_ref[0]
    cmask = (_iota((1, n_tok), 1) * CMP_STRIDE + (2 * CMP_STRIDE - 1)) <= past
    row8 = _iota((8, 1), 0)
    n_sb = past // SEL_BLOCK + 1
    cur = past // SEL_BLOCK
    n_l = mw_ref.shape[1]
    jl = _iota((1, n_l), 1)
    for g in range(N_KV):
        p = _softmax_rows(_dot_nt(_q8(q, g), kc, HI), cmask)
        o_ref[0, g] = _dot(p.astype(BF16), vc)
        imp = jnp.sum(jnp.where(row8 < REP, p, 0.0), axis=0, keepdims=True)
        pslc = _dot(jnp.broadcast_to(imp, (8, n_tok)), mw_ref[...], HI)[0:1, :]
        forced = (jl == 0) | (jl == cur) | (jl == cur - 1)
        score = jnp.where(jl >= n_sb, -jnp.inf, jnp.where(forced, BIG, jnp.where(jl <= cur, pslc, NEG)))
        rank = _rank_lanes(score)
        kk = _iota((N_SEL, n_l), 0)
        idx = jnp.sum(jnp.where(rank == kk, _iota((N_SEL, n_l), 1), 0), axis=1, keepdims=True)
        idx_ref[0, g] = jnp.broadcast_to(idx, (N_SEL, LANES))


def _cmp_sample(layer, pt, cache_t, new, q, wbig, w2, pe, mw, past):
    bd, n_pages = pt.shape
    n_tok = n_pages * (PAGE // CMP_STRIDE)
    blk = lambda *s: pl.BlockSpec((1,) + s, lambda b_, pt_: (b_,) + (0,) * len(s))
    cst = lambda shape: pl.BlockSpec(shape, lambda b_, pt_: (0,) * len(shape), pipeline_mode=pl.Buffered(1))
    gs = pltpu.PrefetchScalarGridSpec(
        num_scalar_prefetch=1, grid=(bd,),
        in_specs=[pl.BlockSpec(memory_space=pl.ANY), blk(1, KV_W), blk(1, QPAD_W),
                  cst((CHUNK_W, 8 * CMP_HIDDEN)), cst((4 * CMP_HIDDEN, KV_W)), cst((8, CHUNK_W)), cst(mw.shape),
                  cst((PAGE, PAGE))],
        out_specs=[blk(N_KV, 8, LANES), blk(N_KV, N_SEL, LANES)],
        scratch_shapes=[pltpu.VMEM((2, n_pages, KV_W, PAGE), F32),
                        pltpu.VMEM((n_tok + CMP_TAIL, CHUNK_W), BF16),
                        pltpu.VMEM((n_tok + CMP_TAIL, 8 * CMP_HIDDEN), F32), pltpu.SemaphoreType.DMA((2,))])
    return pl.pallas_call(
        functools.partial(_cmp_sample_kernel, layer=layer, n_pages=n_pages, past=past),
        grid_spec=gs,
        out_shape=[jax.ShapeDtypeStruct((bd, N_KV, 8, LANES), F32), jax.ShapeDtypeStruct((bd, N_KV, N_SEL, LANES), I32)],
        compiler_params=_params(1, 60 * 1024 * 1024),
        name="cmp_sample",
    )(pt, cache_t, new, q, wbig, w2, pe, mw, jnp.asarray(_chunk_perm(), BF16))


def _chunk_perm():
    per_page = PAGE // CMP_STRIDE
    src = np.arange(PAGE)
    dst = (src % CMP_STRIDE) * per_page + src // CMP_STRIDE
    m = np.zeros((PAGE, PAGE), np.float32)
    m[src, dst] = 1.0
    return m


def _one_query_attention(q8, tiles, k_new, v_new):
    qb = q8.astype(BF16)
    ss = []
    for k_t, _, mask in tiles:
        s = _dot(qb, k_t)
        ss.append(s if mask is None else jnp.where(mask, s, NEG))
    s_new = jnp.sum(q8 * k_new, axis=-1, keepdims=True)
    m = functools.reduce(jnp.maximum, [jnp.max(s, axis=-1, keepdims=True) for s in ss] + [s_new])
    es = [jnp.exp(s - m) for s in ss]
    e_new = jnp.exp(s_new - m)
    l = functools.reduce(jnp.add, [jnp.sum(e, axis=-1, keepdims=True) for e in es]) + e_new
    acc = functools.reduce(jnp.add, [_dot_nt(e.astype(BF16), t[1]) for e, t in zip(es, tiles)])
    return (acc + e_new * v_new) / l


def _selwin_sample_kernel(pt_ref, idx_ref, cache_ref, newsel_ref, win_ref, newwin_ref, q_ref,
                          osel_ref, owin_ref, buf_ref, sem, *, layer, n_cb):
    b = pl.program_id(0)
    per_page = PAGE // SEL_BLOCK

    def copy(g, k):
        blk = jnp.minimum(idx_ref[b, g * N_SEL + k], n_cb - 1)
        return pltpu.make_async_copy(cache_ref.at[layer, pt_ref[b, blk // per_page]], buf_ref.at[g, k], sem.at[0])

    for g in range(N_KV):
        for k in range(N_SEL):
            copy(g, k).start()
    q = q_ref[0]
    new_sel = newsel_ref[0]
    new_win = newwin_ref[0]
    win_k = win_ref[0, 0, :LANES, :].astype(BF16)
    win_v = win_ref[0, 0, LANES:, :].astype(BF16)
    for g in range(N_KV):
        owin_ref[0, g] = _one_query_attention(_q8(q, g), [(win_k, win_v, None)], new_win[:, :LANES], new_win[:, LANES:])
    for g in range(N_KV):
        for k in range(N_SEL):
            copy(g, k).wait()
    lane_blk = _iota((1, PAGE), 1) // SEL_BLOCK
    for g in range(N_KV):
        tiles = []
        for k in range(N_SEL):
            blk = idx_ref[b, g * N_SEL + k]
            mask = (lane_blk == blk % per_page) & (blk < n_cb)
            tiles.append((buf_ref[g, k, :LANES, :].astype(BF16), buf_ref[g, k, LANES:, :].astype(BF16), mask))
        osel_ref[0, g] = _one_query_attention(_q8(q, g), tiles, new_sel[:, :LANES], new_sel[:, LANES:])


def _selwin_sample(layer, pt, idx, cache_t, newsel, win_t, newwin, q, past):
    bd = pt.shape[0]
    wb = win_t.shape[-1]
    blk = lambda *s: pl.BlockSpec((1,) + s, lambda b_, *_: (b_,) + (0,) * len(s))
    gs = pltpu.PrefetchScalarGridSpec(
        num_scalar_prefetch=2, grid=(bd,),
        in_specs=[pl.BlockSpec(memory_space=pl.ANY), blk(1, KV_W),
                  pl.BlockSpec((1, 1, KV_W, wb), lambda b_, *_: (layer, b_, 0, 0)), blk(1, KV_W), blk(1, QPAD_W)],
        out_specs=[blk(N_KV, 8, LANES), blk(N_KV, 8, LANES)],
        scratch_shapes=[pltpu.VMEM((N_KV, N_SEL, KV_W, PAGE), F32), pltpu.SemaphoreType.DMA((1,))])
    return pl.pallas_call(
        functools.partial(_selwin_sample_kernel, layer=layer, n_cb=past // SEL_BLOCK),
        grid_spec=gs,
        out_shape=[jax.ShapeDtypeStruct((bd, N_KV, 8, LANES), F32)] * 2,
        compiler_params=_params(1),
        name="selwin_sample",
    )(pt, idx, cache_t, newsel, win_t, newwin, q)


MOBA_CHUNK_PAGES = 8


def _moba_sample_kernel(pt_ref, cache_ref, new_ref, q_ref, o_ref, x_ref, s_ref, sem, *, layer, n_pages, n_blk):
    slot = _paged_prologue(lambda b, p: cache_ref.at[layer, pt_ref[b, p]], lambda sl, p: x_ref.at[sl, p], sem, n_pages)
    q = q_ref[0]
    new = new_ref[0]
    per_blk = MOBA_BLOCK // PAGE
    lane_sq = _iota((LANES, LANES), 1)
    km_t = jnp.zeros((LANES, LANES), F32)
    for j in range(n_blk):
        blk = functools.reduce(jnp.add, [x_ref[slot, j * per_blk + t, :LANES, :] for t in range(per_blk)])
        km_t = jnp.where(lane_sq == j, jnp.sum(blk, axis=1, keepdims=True) * (1.0 / MOBA_BLOCK), km_t)
    jl = _iota((1, LANES), 1)
    row8 = _iota((8, LANES), 0)
    q8s, notsels = [], []
    for g in range(N_KV):
        q8 = _q8(q, g)
        gate = _dot(q8, km_t, HI)
        notsel = jnp.ones((8, LANES), F32)
        for r in range(REP):
            score = jnp.where(jl < n_blk, gate[r:r + 1, :], -jnp.inf)
            sel = (_rank_lanes(score) < MOBA_TOPK) & (jl < n_blk)
            notsel = jnp.where((row8 == r) & sel, 0.0, notsel)
        q8s.append(q8)
        notsels.append(notsel)
    cw = MOBA_CHUNK_PAGES * PAGE
    blk_per_chunk = cw // MOBA_BLOCK
    blk_of_lane = _iota((1, cw), 1) // MOBA_BLOCK
    n_chunks = n_pages // MOBA_CHUNK_PAGES
    for c in range(n_chunks):
        k_t = jnp.concatenate([x_ref[slot, c * MOBA_CHUNK_PAGES + t, :LANES, :] for t in range(MOBA_CHUNK_PAGES)],
                              axis=1).astype(BF16)
        for g in range(N_KV):
            off = jnp.zeros((8, cw), F32)
            for t in range(blk_per_chunk):
                j = c * blk_per_chunk + t
                off = jnp.where(blk_of_lane == t, notsels[g][:, j:j + 1], off)
            s_ref[g, :, c * cw:(c + 1) * cw] = jnp.where(off > 0.5, NEG, _dot(q8s[g].astype(BF16), k_t))
    ls, e_news = [], []
    for g in range(N_KV):
        s = s_ref[g]
        s_new = jnp.sum(q8s[g] * new[:, :LANES], axis=-1, keepdims=True)
        m = jnp.maximum(jnp.max(s, axis=-1, keepdims=True), s_new)
        e = jnp.exp(s - m)
        e_new = jnp.exp(s_new - m)
        ls.append(jnp.sum(e, axis=-1, keepdims=True) + e_new)
        e_news.append(e_new)
        s_ref[g] = e
    acc = [e_news[g] * new[:, LANES:] for g in range(N_KV)]
    for c in range(n_chunks):
        v_t = jnp.concatenate([x_ref[slot, c * MOBA_CHUNK_PAGES + t, LANES:, :] for t in range(MOBA_CHUNK_PAGES)],
                              axis=1).astype(BF16)
        for g in range(N_KV):
            acc[g] = acc[g] + _dot_nt(s_ref[g, :, c * cw:(c + 1) * cw].astype(BF16), v_t)
    for g in range(N_KV):
        o_ref[0, g] = acc[g] / ls[g]


def _moba_sample(layer, pt, cache_t, new, q, past):
    bd, n_pages = pt.shape
    n_blk = past // MOBA_BLOCK
    blk = lambda *s: pl.BlockSpec((1,) + s, lambda b_, pt_: (b_,) + (0,) * len(s))
    gs = pltpu.PrefetchScalarGridSpec(
        num_scalar_prefetch=1, grid=(bd,),
        in_specs=[pl.BlockSpec(memory_space=pl.ANY), blk(1, KV_W), blk(1, QPAD_W)],
        out_specs=blk(N_KV, 8, LANES),
        scratch_shapes=[pltpu.VMEM((2, n_pages, KV_W, PAGE), F32), pltpu.VMEM((N_KV, 8, past), F32),
                        pltpu.SemaphoreType.DMA((2,))])
    return pl.pallas_call(
        functools.partial(_moba_sample_kernel, layer=layer, n_pages=n_pages, n_blk=n_blk),
        grid_spec=gs,
        out_shape=jax.ShapeDtypeStruct((bd, N_KV, 8, LANES), F32),
        compiler_params=_params(1),
        name="moba_sample",
    )(pt, cache_t, new, q)


def _combine_sample_kernel(u_ref, st_ref, w_ref, b_ref, lg_ref, lb_ref, gates_ref, ocmp_ref, osel_ref, owin_ref,
                           omoba_ref, conv_ref, nsa_ref, moba_ref):
    y = u_ref[...] * w_ref[CONV_WIDTH - 1:CONV_WIDTH, :]
    for j in range(CONV_WIDTH - 1):
        y = y + st_ref[j] * w_ref[j:j + 1, :]
    conv_ref[...] = _conv_post(y, b_ref, lg_ref, lb_ref).astype(BF16)
    gates = gates_ref[...]
    w = N_HEADS * HEAD_DIM
    head_of_lane = _iota((1, w), 1) // HEAD_DIM
    acc = jnp.zeros(ocmp_ref.shape, F32)
    for br, ref in enumerate((ocmp_ref, osel_ref, owin_ref)):
        ge = jnp.zeros(ocmp_ref.shape, F32)
        for hh in range(N_HEADS):
            c = br * N_HEADS + hh
            ge = jnp.where(head_of_lane == hh, gates[:, c:c + 1], ge)
        acc = acc + ge * ref[...]
    nsa_ref[...] = acc.astype(BF16)
    moba_ref[...] = omoba_ref[...].astype(BF16)


def _combine_sample(u, st_t, w, b, lg, lb, gates, o_cmp, o_sel, o_win, o_moba):
    bd = u.shape[0]
    w_h = N_HEADS * HEAD_DIM
    return pl.pallas_call(
        _combine_sample_kernel,
        out_shape=[jax.ShapeDtypeStruct((bd, CONV_CH), BF16), jax.ShapeDtypeStruct((bd, w_h), BF16),
                   jax.ShapeDtypeStruct((bd, w_h), BF16)],
        name="combine_sample",
    )(u, st_t, w, b, lg, lb, gates, o_cmp, o_sel, o_win, o_moba)


def _rope_tables(pos):
    half = HEAD_DIM // 2
    inv = ROPE_THETA ** (-jnp.arange(half, dtype=F32) / half)
    ang = pos.astype(F32)[:, None] * inv[None, :]
    cos, sin = jnp.cos(ang), jnp.sin(ang)
    zero = jnp.zeros_like(sin)
    return (jnp.tile(cos, (1, 4)), jnp.tile(jnp.concatenate([-sin, zero], axis=1), (1, 2)),
            jnp.tile(jnp.concatenate([zero, sin], axis=1), (1, 2)))


def _pad_heads(w):
    d = w.shape[:-1]
    w = w.reshape(*d, N_HEADS, HEAD_DIM)
    return jnp.pad(w, [(0, 0)] * len(d) + [(0, 0), (0, LANES - HEAD_DIM)]).reshape(*d, QPAD_W)


def _relayout_w_in(w_in):
    c_q = 2 * CONV_CH
    c_kv = c_q + N_HEADS * HEAD_DIM
    c_g = c_kv + 6 * N_KV * HEAD_DIM
    c_qm = c_g + 3 * N_HEADS
    c_kvm = c_qm + N_HEADS * HEAD_DIM
    gates = jnp.pad(w_in[..., c_g:c_qm], ((0, 0), (0, 0), (0, LANES - 3 * N_HEADS)))
    return jnp.concatenate([w_in[..., :c_q], _pad_heads(w_in[..., c_q:c_kv]), w_in[..., c_kv:c_g],
                            _pad_heads(w_in[..., c_qm:c_kvm]), w_in[..., c_kvm:], gates], axis=-1).astype(BF16)


def _compress_weights(pe_k, pe_v, wk1, wk2, wv1, wv2):
    depth = wk1.shape[0]
    eye = jnp.eye(4, dtype=F32)
    w1 = jnp.stack([wk1, wk1, wv1, wv1], axis=1).reshape(depth, 4, 2, CMP_STRIDE, HEAD_DIM, CMP_HIDDEN)
    wbig = jnp.einsum('lsaidh,st->lisdath', w1, eye).reshape(depth, CHUNK_W, 8 * CMP_HIDDEN).astype(BF16)
    w2 = jnp.stack([wk2, wk2, wv2, wv2], axis=1)
    w2big = jnp.einsum('lshd,st->lshtd', w2, eye).reshape(depth, 4 * CMP_HIDDEN, KV_W).astype(BF16)
    pe = jnp.stack([pe_k, pe_k, pe_v, pe_v], axis=1).reshape(depth, 4, 2, CMP_STRIDE, HEAD_DIM)
    pe = pe.transpose(0, 2, 3, 1, 4).reshape(depth, 2, CHUNK_W)
    return wbig, w2big, jnp.pad(pe, ((0, 0), (0, 6), (0, 0)))


def _slc_matrix(n_c, n_sb):
    ratio = SEL_BLOCK // CMP_STRIDE
    c = np.arange(n_c)[:, None]
    j = np.arange(n_sb)[None, :]
    m = np.where(c == ratio * j, 1.0, 0.0) + np.where((c > ratio * j) & (c < ratio * (j + 1)), 2.0, 0.0) \
        + np.where(c == ratio * (j + 1), 1.0, 0.0)
    return m.astype(np.float32)


def _heads_from_q8(o):
    return jnp.concatenate([o[:, 0, :REP, :HEAD_DIM], o[:, 1, :REP, HEAD_DIM:]], axis=1).reshape(o.shape[0], -1)


def kernel(x_prompt, x_sample, cache_nsa_cmp, cache_nsa_sel, cache_moba, state_nsa_win, state_conv, page_table,
           g_mix, w_in, conv_w, conv_b, conv_ln_g, conv_ln_b, cmp_pe_k, cmp_pe_v, cmp_wk1, cmp_wk2, cmp_wv1, cmp_wv2,
           w_out, g_ffn, w_gate_up, w_down, g_final):
    nb, seq, _ = x_prompt.shape
    bd, dec_seq, _ = x_sample.shape
    depth = g_mix.shape[0]
    n_pages = page_table.shape[1]
    past = n_pages * PAGE
    assert dec_seq == 1 and seq % 512 == 0 and seq // SEL_BLOCK <= HEAD_DIM and 3 * (seq // MOBA_BLOCK) <= HEAD_DIM
    assert past % MOBA_BLOCK == 0 and past // MOBA_BLOCK <= LANES and bd % 8 == 0 and n_pages % MOBA_CHUNK_PAGES == 0
    assert n_pages % (2 * CMP_PAIRS_PER_TRIP) == 0
    n_p = nb * seq
    tm = 512

    w_in_r = _relayout_w_in(w_in)
    wbig, w2big, pe_rows = _compress_weights(cmp_pe_k, cmp_pe_v, cmp_wk1, cmp_wk2, cmp_wv1, cmp_wv2)
    w_out_b, w_gu_b, w_down_b = w_out.astype(BF16), w_gate_up.astype(BF16), w_down.astype(BF16)
    conv_w_p = jnp.pad(conv_w, ((0, 0), (0, CONV_HALO - CONV_WIDTH), (0, 0)))
    tab_p = _rope_tables(jnp.arange(seq))
    tab_s = _rope_tables(jnp.full((bd,), past))
    n_c = seq // CMP_STRIDE
    mwt_p = jnp.asarray(_slc_matrix(n_c, seq // SEL_BLOCK).T)
    n_sb_s = past // SEL_BLOCK + 1
    mw_s = jnp.asarray(np.pad(_slc_matrix(past // CMP_STRIDE, n_sb_s), ((0, 0), (0, -n_sb_s % LANES))))

    feat_major = lambda a: a.transpose(0, 1, 3, 4, 5, 2).reshape(a.shape[0], a.shape[1], KV_W, a.shape[2])
    cmp_t, sel_t, moba_t, win_t = (feat_major(a) for a in (cache_nsa_cmp, cache_nsa_sel, cache_moba, state_nsa_win))

    xp = x_prompt.reshape(n_p, D_MODEL)
    xs = x_sample.reshape(bd, D_MODEL)
    new_p = [[], [], [], [], []]
    new_s = [[], [], [], [], []]
    r2 = lambda a: a.reshape(1, -1)
    for l in range(depth):
        (u, q, cmp_rows, sel_rows, win_rows, moba_rows, qm, gates,
         selk, selv, wink, winv, mobak, mobav) = _proj(xp, r2(g_mix[l]), w_in_r[l], *tab_p, seq, tm)
        conv_o = _conv(u, conv_w_p[l], r2(conv_b[l]), r2(conv_ln_g[l]), r2(conv_ln_b[l]), nb, seq, 512)
        comp = _compress_prompt(cmp_rows, wbig[l], w2big[l], pe_rows[l], nb, seq)
        nsa_o = _nsa_prompt(q, gates, comp, selk, selv, wink, winv, mwt_p, nb, seq)
        moba_o = _moba_prompt(qm, moba_rows, mobak, mobav, nb, seq)
        xp = _ffn(xp, conv_o, nsa_o, moba_o, w_out_b[l], r2(g_ffn[l]), w_gu_b[l], w_down_b[l], tm)
        kv6 = lambda a, n_: a.reshape(n_, -1, 2, N_KV, HEAD_DIM)
        new_p[0].append(kv6(cmp_rows, nb))
        new_p[1].append(kv6(sel_rows, nb))
        new_p[2].append(kv6(moba_rows, nb))
        new_p[3].append(kv6(win_rows, nb)[:, -min(WINDOW, seq):])
        new_p[4].append(u.reshape(nb, seq, CONV_CH)[:, -(CONV_WIDTH - 1):])

        (u, q, cmp_rows, sel_rows, win_rows, moba_rows, qm, gates, *_) = _proj(xs, r2(g_mix[l]), w_in_r[l], *tab_s, bd, bd)
        b3 = lambda a: a.reshape(bd, 1, -1)
        o_cmp, idx = _cmp_sample(l, page_table, cmp_t, b3(cmp_rows), b3(q), wbig[l], w2big[l], pe_rows[l],
                                 mw_s, past)
        o_sel, o_win = _selwin_sample(l, page_table, idx[:, :, :, 0].reshape(bd, N_KV * N_SEL), sel_t,
                                      b3(sel_rows), win_t, b3(win_rows), b3(q), past)
        o_moba = _moba_sample(l, page_table, moba_t, b3(moba_rows), b3(qm), past)
        conv_o, nsa_o, moba_o = _combine_sample(
            u, state_conv[l].transpose(1, 0, 2), conv_w_p[l], r2(conv_b[l]), r2(conv_ln_g[l]), r2(conv_ln_b[l]), gates,
            _heads_from_q8(o_cmp), _heads_from_q8(o_sel), _heads_from_q8(o_win), _heads_from_q8(o_moba))
        xs = _ffn(xs, conv_o, nsa_o, moba_o, w_out_b[l], r2(g_ffn[l]), w_gu_b[l], w_down_b[l], bd)
        new_s[0].append(kv6(cmp_rows, bd))
        new_s[1].append(kv6(sel_rows, bd))
        new_s[2].append(kv6(moba_rows, bd))
        new_s[3].append(jnp.concatenate([state_nsa_win[l], kv6(win_rows, bd)], axis=1)[:, -state_nsa_win.shape[2]:])
        new_s[4].append(jnp.concatenate([state_conv[l], u[:, None, :]], axis=1)[:, -(CONV_WIDTH - 1):])

    y_prompt = _final_norm(xp, r2(g_final), tm).reshape(nb, seq, D_MODEL)
    y_sample = _final_norm(xs, r2(g_final), bd).reshape(bd, 1, D_MODEL)
    return (y_prompt, y_sample,
            jnp.stack(new_p[0]), jnp.stack(new_s[0]),
            jnp.stack(new_p[1]), jnp.stack(new_s[1]),
            jnp.stack(new_p[2]), jnp.stack(new_s[2]),
            jnp.stack(new_p[3]), jnp.stack(new_s[3]),
            jnp.stack(new_p[4]), jnp.stack(new_s[4]))
```

```python
import functools

import numpy as np
import jax
import jax.numpy as jnp
from jax import lax
from jax.experimental import pallas as pl
from jax.experimental.pallas import tpu as pltpu

F32 = jnp.float32
BF16 = jnp.bfloat16
I32 = jnp.int32

D_MODEL = 1024
HEAD_DIM = 64
LANES = 128
CONV_CH = 256
N_HEADS = 6
N_KV = 2
REP = N_HEADS // N_KV
CONV_WIDTH = 31
CMP_STRIDE = 16
CMP_HIDDEN = 128
SEL_BLOCK = 64
N_SEL = 16
WINDOW = 512
BAND = 128
MOBA_BLOCK = 256
MOBA_TOPK = 3
PAGE = 128
FFN_HIDDEN = 2816
ROPE_THETA = 10000.0
EPS = 1e-6
NEG = -1e30
BIG = 1e30
ATTN_SCALE = HEAD_DIM ** -0.5
HI = lax.Precision.HIGHEST

KV_W = 2 * N_KV * HEAD_DIM
QPAD_W = N_HEADS * LANES
CHUNK_W = CMP_STRIDE * KV_W

OFF_U, OFF_Q, OFF_KV, OFF_QM, OFF_KVM, OFF_G, W_IN_COLS = 0, 512, 1280, 2048, 2816, 3072, 3200

VMEM_LIMIT = 56 * 1024 * 1024


def _params(n_axes, limit=VMEM_LIMIT):
    return pltpu.CompilerParams(dimension_semantics=("arbitrary",) * n_axes, vmem_limit_bytes=limit)


def _const_spec(shape):
    nd = len(shape)
    return pl.BlockSpec(shape, lambda *_: (0,) * nd, pipeline_mode=pl.Buffered(1))


def _dot(a, b, precision=None):
    return jnp.dot(a, b, preferred_element_type=F32, precision=precision)


def _dot_nt(a, b, precision=None):
    return lax.dot_general(a, b, (((1,), (1,)), ((), ())), preferred_element_type=F32, precision=precision)


def _iota(shape, dim):
    return lax.broadcasted_iota(I32, shape, dim)


def _rms(x, g):
    return x * lax.rsqrt(jnp.mean(x * x, axis=-1, keepdims=True) + EPS) * g


def _compact_heads(a):
    lo = _iota(a[0].shape, 1) < HEAD_DIM
    return jnp.concatenate([
        jnp.where(lo, a[0], pltpu.roll(a[1], HEAD_DIM, 1)),
        jnp.where(lo, a[2], a[3]),
        jnp.where(lo, pltpu.roll(a[4], HEAD_DIM, 1), a[5])], axis=1)


def _proj_kernel(x_ref, g_ref, w_ref, cos_ref, sa_ref, sb_ref, u_ref, q_ref, qm_ref, gates_ref, *refs,
                 seq, tm, prompt):
    if prompt:
        (cmp_ref, moba_ref, selk_ref, selv_ref, wink_ref, winv_ref, mobak_ref, mobav_ref,
         cmp_t_ref, sel_t_ref, win_t_ref, moba_t_ref) = refs
        sel_ref = win_ref = None
    else:
        cmp_ref, sel_ref, win_ref, moba_ref = refs
        cmp_t_ref = sel_t_ref = win_t_ref = moba_t_ref = None
    i = pl.program_id(0)
    h = _rms(x_ref[...], g_ref[...]).astype(BF16)
    cos, sa, sb = cos_ref[...], sa_ref[...], sb_ref[...]

    def seg(off, width):
        return _dot(h, w_ref[:, off:off + width])

    def rope(z):
        return z * cos + pltpu.roll(z, LANES - 32, 1) * sa + pltpu.roll(z, 32, 1) * sb

    zu = seg(OFF_U, 2 * CONV_CH)
    u_ref[...] = zu[:, :CONV_CH] * jax.nn.sigmoid(zu[:, CONV_CH:])

    for off, ref in ((OFF_Q, q_ref), (OFF_QM, qm_ref)):
        z = seg(off, QPAD_W)
        for hh in range(N_HEADS):
            ref[:, hh * LANES:(hh + 1) * LANES] = rope(z[:, hh * LANES:(hh + 1) * LANES]) * ATTN_SCALE

    t = (i * tm) % seq + _iota((tm, LANES), 0)
    lane = _iota((tm, LANES), 1)
    lo = lane < HEAD_DIM
    oh_sel = jnp.where((lane >= HEAD_DIM) & ((t >> 6) == lane - HEAD_DIM), NEG, 0.0)
    oh_moba = jnp.where((lane >= HEAD_DIM) & (lane < HEAD_DIM + REP * MOBA_LANES)
                        & ((t >> 8) == ((lane - HEAD_DIM) & (MOBA_LANES - 1))), NEG, 0.0)

    def rows(off, ref, t_ref):
        z = seg(off, KV_W)
        k = rope(z[:, :LANES])
        v = z[:, LANES:]
        if ref is not None:
            ref[:, :LANES] = k
            ref[:, LANES:] = v
        if t_ref is not None:
            t_ref[0, :LANES, :] = k.T
            t_ref[0, LANES:, :] = v.T
        return k, v

    rows(OFF_KV, cmp_ref, cmp_t_ref)
    k, v = rows(OFF_KV + KV_W, sel_ref, sel_t_ref)
    if prompt:
        selk_ref[:, :LANES] = jnp.where(lo, k, oh_sel).astype(BF16)
        selk_ref[:, LANES:] = jnp.where(lo, pltpu.roll(k, HEAD_DIM, 1), oh_sel).astype(BF16)
        selv_ref[...] = v.astype(BF16)
    k, v = rows(OFF_KV + 2 * KV_W, win_ref, win_t_ref)
    if prompt:
        wink_ref[:, :LANES] = k.astype(BF16)
        wink_ref[:, LANES:] = pltpu.roll(k, HEAD_DIM, 1).astype(BF16)
        winv_ref[...] = v.astype(BF16)
    k, v = rows(OFF_KVM, moba_ref, moba_t_ref)
    if prompt:
        mobak_ref[:, :LANES] = jnp.where(lo, k, oh_moba).astype(BF16)
        mobak_ref[:, LANES:] = jnp.where(lo, pltpu.roll(k, HEAD_DIM, 1), oh_moba).astype(BF16)
        mobav_ref[...] = v.astype(BF16)

    gates_ref[...] = jax.nn.sigmoid(seg(OFF_G, LANES))


def _proj(x, g, w, cos, sa, sb, seq, tm, prompt):
    n = x.shape[0]
    nt = cos.shape[0] // tm
    row = lambda wd: pl.BlockSpec((tm, wd), lambda i: (i, 0))
    tab = pl.BlockSpec((tm, LANES), lambda i: (i % nt, 0))
    shapes = [((n, wd), F32) for wd in (CONV_CH, QPAD_W, QPAD_W, LANES)]
    specs = [row(s[1]) for s, _ in shapes]
    if prompt:
        extra = [((n, KV_W), F32)] * 2 + [((n, wd), BF16) for wd in (2 * LANES, LANES) * 3]
        shapes += extra + [((n // seq, KV_W, seq), F32)] * 4
        specs += [row(s[1]) for s, _ in extra]
        specs += [pl.BlockSpec((1, KV_W, tm), lambda i: (i // nt, 0, i % nt))] * 4
    else:
        shapes += [((n, KV_W), F32)] * 4
        specs += [row(KV_W)] * 4
    return pl.pallas_call(
        functools.partial(_proj_kernel, seq=seq, tm=tm, prompt=prompt),
        grid=(n // tm,),
        in_specs=[row(D_MODEL), _const_spec((1, D_MODEL)), _const_spec((D_MODEL, W_IN_COLS)), tab, tab, tab],
        out_specs=specs,
        out_shape=[jax.ShapeDtypeStruct(s, d) for s, d in shapes],
        compiler_params=_params(1),
        name="proj",
    )(x, g, w, cos, sa, sb)


CONV_HALO = 32


def _conv_post(y, b_ref, lg_ref, lb_ref):
    y = y + b_ref[...]
    mu = jnp.mean(y, axis=-1, keepdims=True)
    var = jnp.mean(jnp.square(y - mu), axis=-1, keepdims=True)
    y = (y - mu) * lax.rsqrt(var + EPS) * lg_ref[...] + lb_ref[...]
    return y * jax.nn.sigmoid(y)


def _conv_kernel(u_ref, prev_ref, w_ref, b_ref, lg_ref, lb_ref, o_ref, ext_ref, *, tq):
    i = pl.program_id(1)
    ext_ref[:CONV_HALO, :] = jnp.where(i > 0, prev_ref[...], 0.0)
    ext_ref[CONV_HALO:, :] = u_ref[...]
    off = CONV_HALO - (CONV_WIDTH - 1)
    y = jnp.zeros((tq, CONV_CH), F32)
    for j in range(CONV_WIDTH):
        y = y + ext_ref[pl.ds(off + j, tq), :] * w_ref[j:j + 1, :]
    o_ref[...] = _conv_post(y, b_ref, lg_ref, lb_ref).astype(BF16)


def _conv(u, w, b, lg, lb, nb, seq, tq):
    n = u.shape[0]
    nq = seq // tq
    per = tq // CONV_HALO
    return pl.pallas_call(
        functools.partial(_conv_kernel, tq=tq),
        grid=(nb, nq),
        in_specs=[pl.BlockSpec((tq, CONV_CH), lambda b_, i: (b_ * nq + i, 0)),
                  pl.BlockSpec((CONV_HALO, CONV_CH), lambda b_, i: (jnp.maximum((b_ * nq + i) * per - 1, 0), 0)),
                  _const_spec((CONV_HALO, CONV_CH)), _const_spec((1, CONV_CH)), _const_spec((1, CONV_CH)),
                  _const_spec((1, CONV_CH))],
        out_specs=pl.BlockSpec((tq, CONV_CH), lambda b_, i: (b_ * nq + i, 0)),
        out_shape=jax.ShapeDtypeStruct((n, CONV_CH), BF16),
        scratch_shapes=[pltpu.VMEM((tq + CONV_HALO, CONV_CH), F32)],
        compiler_params=_params(2),
        name="conv",
    )(u, u, w, b, lg, lb)


def _gelu_tanh(x):
    return x * (0.5 * (1.0 + jnp.tanh(0.7978845608028654 * (x + 0.044715 * (x * x * x)))))


CMP_TAIL = 16
CMP_PAIRS_PER_TRIP = 4


HALF_CHUNK_W = CHUNK_W // 2
CMP_AB_W = 4 * CMP_HIDDEN


def _compress(xb_ref, n_tok, wkv_ref, w2_ref, pe_ref, ab_ref):
    half = CMP_AB_W // 2
    hid = []
    for kv in range(2):
        ab_ref[kv] = _dot(xb_ref[kv], wkv_ref[kv])
        pw = _dot(pe_ref[kv].astype(BF16), wkv_ref[kv])
        pe_all = pw[0:1, :half] + pw[1:2, half:]
        hid.append(ab_ref[kv, pl.ds(0, n_tok), :half] + ab_ref[kv, pl.ds(1, n_tok), half:] + pe_all)
    return _dot(_gelu_tanh(jnp.concatenate(hid, axis=1)).astype(BF16), w2_ref[...])


def _compress_kernel(x_ref, wkv_ref, w2_ref, pe_ref, o_ref, x_scr, ab_ref, *, n_tok):
    x = x_ref[0]
    for kv in range(2):
        x_scr[kv, pl.ds(0, n_tok), :] = jnp.concatenate(
            [x[:, i * KV_W + kv * LANES:i * KV_W + (kv + 1) * LANES] for i in range(CMP_STRIDE)], axis=1).astype(BF16)
        x_scr[kv, pl.ds(n_tok, CMP_TAIL), :] = jnp.zeros((CMP_TAIL, HALF_CHUNK_W), BF16)
    o_ref[0] = _compress(x_scr, n_tok, wkv_ref, w2_ref, pe_ref, ab_ref)


def _compress_prompt(rows, wkv, w2, pe, nb, seq):
    n_tok = seq // CMP_STRIDE
    x = rows.reshape(nb, n_tok, CHUNK_W)
    return pl.pallas_call(
        functools.partial(_compress_kernel, n_tok=n_tok),
        grid=(nb,),
        in_specs=[pl.BlockSpec((1, n_tok, CHUNK_W), lambda b_: (b_, 0, 0)),
                  _const_spec((2, HALF_CHUNK_W, CMP_AB_W)), _const_spec((4 * CMP_HIDDEN, KV_W)),
                  _const_spec((2, 8, HALF_CHUNK_W))],
        out_specs=pl.BlockSpec((1, n_tok, KV_W), lambda b_: (b_, 0, 0)),
        out_shape=jax.ShapeDtypeStruct((nb, n_tok, KV_W), F32),
        scratch_shapes=[pltpu.VMEM((2, n_tok + CMP_TAIL, HALF_CHUNK_W), BF16),
                        pltpu.VMEM((2, n_tok + CMP_TAIL, CMP_AB_W), F32)],
        compiler_params=_params(1),
        name="compress_prompt",
    )(x, wkv, w2, pe)


def _softmax_rows(s, mask):
    s = jnp.where(mask, s, NEG)
    m = jnp.max(s, axis=-1, keepdims=True)
    e = jnp.where(mask, jnp.exp(s - m), 0.0)
    l = jnp.sum(e, axis=-1, keepdims=True)
    return e / jnp.where(l > 0.0, l, 1.0)


def _rank_rows(score, n):
    jj = _iota(score.shape, 0)
    rank = jnp.zeros(score.shape, I32)
    for ii in range(n):
        row = score[ii:ii + 1, :]
        rank = rank + jnp.where((row > score) | ((row == score) & (ii < jj)), 1, 0)
    return rank


def _flash_step(carry, s, v):
    m, l, acc = carry
    m_new = jnp.maximum(m, jnp.max(s, axis=-1, keepdims=True))
    a = jnp.exp(m - m_new)
    p = jnp.exp(s - m_new)
    return m_new, a * l + jnp.sum(p, axis=-1, keepdims=True), a * acc + _dot(p.astype(BF16), v)


def _flash_init(rows):
    return jnp.full((rows, 1), NEG, F32), jnp.zeros((rows, 1), F32), jnp.zeros((rows, LANES), F32)


NSA_TQ = 256
SEL_TK = 512


def _nsa_kernel(q_ref, gates_ref, comp_ref, selk_ref, selv_ref, wink_ref, winv_ref, mwt_ref, o_ref, *, tq):
    i = pl.program_id(1)
    t0 = i * tq
    q = q_ref[...]
    comp = comp_ref[0]
    n_c = comp.shape[0]
    kc = comp[:, :LANES]
    kc_g = (kc, pltpu.roll(kc, HEAD_DIM, 1))
    vc = comp[:, LANES:].astype(BF16)
    pos_col = t0 + _iota((tq, 1), 0)
    cmask = (_iota((1, n_c), 1) * CMP_STRIDE + (2 * CMP_STRIDE - 1)) <= pos_col
    n_sb = mwt_ref.shape[0]

    def head(hh):
        return q[:, hh * LANES:(hh + 1) * LANES]

    o_cmp, ns = [], []
    for g in range(N_KV):
        imp = jnp.zeros((tq, n_c), F32)
        for r in range(REP):
            p = _softmax_rows(_dot_nt(head(g * REP + r), kc_g[g], HI), cmask)
            imp = imp + p
            o_cmp.append(_dot(p.astype(BF16), vc))
        pslc_t = _dot_nt(mwt_ref[...], imp, HI)
        jj = _iota((n_sb, tq), 0)
        cur = (t0 + _iota((n_sb, tq), 1)) >> 6
        forced = (jj == 0) | (jj == cur) | (jj == cur - 1)
        score = jnp.where(forced, BIG, jnp.where(jj <= cur, pslc_t, NEG))
        notsel = jnp.where(_rank_rows(score, n_sb) < N_SEL, 0.0, 1.0)
        parts = [jnp.zeros((HEAD_DIM, tq), F32), notsel]
        if n_sb < HEAD_DIM:
            parts.append(jnp.zeros((HEAD_DIM - n_sb, tq), F32))
        ns.append(jnp.concatenate(parts, axis=0).T.astype(BF16))

    row_pos = t0 + (_iota((REP * tq, 1), 0) % tq)

    qas = [jnp.concatenate([head(g * REP + r).astype(BF16) + ns[g] for r in range(REP)], axis=0) for g in range(N_KV)]

    def tile(kt):
        k0 = pl.multiple_of(kt * SEL_TK, SEL_TK)
        v = selv_ref[pl.ds(k0, SEL_TK), :]
        return [_dot_nt(qas[g], selk_ref[pl.ds(k0, SEL_TK), g * LANES:(g + 1) * LANES]) for g in range(N_KV)], v

    def body(kt, carry):
        ss, v = tile(kt)
        return tuple(_flash_step(carry[g], ss[g], v) for g in range(N_KV))

    last = t0 // SEL_TK
    carry = lax.fori_loop(0, last, body, tuple(_flash_init(REP * tq) for _ in range(N_KV)))
    ss, v = tile(last)
    kpos = last * SEL_TK + _iota((1, SEL_TK), 1)
    o_sel = []
    for g in range(N_KV):
        m, l, acc = _flash_step(carry[g], jnp.where(kpos <= row_pos, ss[g], NEG), v)
        o = acc / l
        o_sel += [o[r * tq:(r + 1) * tq] for r in range(REP)]

    o_win = []
    n_w = (WINDOW + tq) // BAND
    cc = _iota((1, BAND), 1)
    rr = _iota((REP * tq, 1), 0) % tq
    for g in range(N_KV):
        qw = jnp.concatenate([head(g * REP + r).astype(BF16) for r in range(REP)], axis=0)
        ss, vs = [], []
        for j in range(n_w):
            k0 = t0 - WINDOW + j * BAND
            ok = k0 >= 0
            k0c = pl.multiple_of(jnp.maximum(k0, 0), BAND)
            s = _dot_nt(qw, wink_ref[pl.ds(k0c, BAND), g * LANES:(g + 1) * LANES])
            if BAND * j < tq:
                ok = ok & (cc >= rr - BAND * j)
            if WINDOW - BAND * j < BAND:
                ok = ok & (cc <= rr + (WINDOW - BAND * j))
            ss.append(jnp.where(ok, s, NEG))
            vs.append(winv_ref[pl.ds(k0c, BAND), :])
        m = functools.reduce(jnp.maximum, [jnp.max(s, axis=-1, keepdims=True) for s in ss])
        es = [jnp.exp(s - m) for s in ss]
        l = functools.reduce(jnp.add, [jnp.sum(e, axis=-1, keepdims=True) for e in es])
        acc = functools.reduce(jnp.add, [_dot(e.astype(BF16), v) for e, v in zip(es, vs)])
        o = acc / l
        o_win += [o[r * tq:(r + 1) * tq] for r in range(REP)]

    gates = gates_ref[...]
    mixed = []
    for hh in range(N_HEADS):
        gc = [gates[:, br * N_HEADS + hh:br * N_HEADS + hh + 1] for br in range(3)]
        mixed.append(gc[0] * o_cmp[hh] + gc[1] * o_sel[hh] + gc[2] * o_win[hh])
    o_ref[...] = _compact_heads(mixed).astype(BF16)


def _nsa_prompt(q, gates, comp, selk, selv, wink, winv, mwt, nb, seq):
    tq = NSA_TQ
    nq = seq // tq
    n = q.shape[0]
    n_c = comp.shape[1]
    row = lambda wd: pl.BlockSpec((tq, wd), lambda b_, i: (b_ * nq + i, 0))
    full = lambda wd: pl.BlockSpec((seq, wd), lambda b_, i: (b_, 0))
    return pl.pallas_call(
        functools.partial(_nsa_kernel, tq=tq),
        grid=(nb, nq),
        in_specs=[row(QPAD_W), row(LANES), pl.BlockSpec((1, n_c, KV_W), lambda b_, i: (b_, 0, 0)),
                  full(2 * LANES), full(LANES), full(2 * LANES), full(LANES), _const_spec(mwt.shape)],
        out_specs=row(N_HEADS * HEAD_DIM),
        out_shape=jax.ShapeDtypeStruct((n, N_HEADS * HEAD_DIM), BF16),
        compiler_params=_params(2),
        name="nsa_prompt",
    )(q, gates, comp, selk, selv, wink, winv, mwt)


MOBA_LANES = 16


def _moba_kernel(q_ref, rows_ref, k_ref, v_ref, o_ref, kmean_ref, *, tq, n_blk):
    i = pl.program_id(1)

    @pl.when(i == 0)
    def _():
        for j in range(n_blk):
            kmean_ref[j:j + 1, :] = jnp.sum(rows_ref[j * MOBA_BLOCK:(j + 1) * MOBA_BLOCK, :LANES],
                                            axis=0, keepdims=True) * (1.0 / MOBA_BLOCK)

    q = q_ref[...]
    km = kmean_ref[...]
    km_g = (km, pltpu.roll(km, HEAD_DIM, 1))
    jj = _iota((n_blk, tq), 0)
    lane = _iota((tq, LANES), 1)
    cc = _iota((1, MOBA_BLOCK), 1)
    rr = _iota((REP * tq, 1), 0) % tq

    def head(hh):
        return q[:, hh * LANES:(hh + 1) * LANES]

    qas, qos = [], []
    for g in range(N_KV):
        ns_rows = [jnp.zeros((HEAD_DIM, tq), F32)]
        for r in range(REP):
            gate = jnp.where(jj < i, _dot_nt(km_g[g], head(g * REP + r), HI), NEG)
            sel = (_rank_rows(gate, n_blk) < MOBA_TOPK) & (jj < i)
            ns_rows.append(jnp.where(sel, 0.0, 1.0))
            if n_blk < MOBA_LANES:
                ns_rows.append(jnp.zeros((MOBA_LANES - n_blk, tq), F32))
        ns_rows.append(jnp.zeros((HEAD_DIM - REP * MOBA_LANES, tq), F32))
        ns = jnp.concatenate(ns_rows, axis=0).T
        qa, qo = [], []
        for r in range(REP):
            mine = (lane >= HEAD_DIM + r * MOBA_LANES) & (lane < HEAD_DIM + (r + 1) * MOBA_LANES)
            qh = head(g * REP + r).astype(BF16)
            qo.append(qh)
            qa.append(qh + jnp.where(mine, ns, 0.0).astype(BF16))
        qas.append(jnp.concatenate(qa, axis=0))
        qos.append(jnp.concatenate(qo, axis=0))

    def scores(k0, width, qs):
        return [_dot_nt(qs[g], k_ref[pl.ds(k0, width), g * LANES:(g + 1) * LANES]) for g in range(N_KV)]

    pair = 2 * MOBA_BLOCK

    def body(kt, carry):
        k0 = pl.multiple_of(kt * pair, pair)
        ss = scores(k0, pair, qas)
        v = v_ref[pl.ds(k0, pair), :]
        return tuple(_flash_step(carry[g], ss[g], v) for g in range(N_KV))

    carry = lax.fori_loop(0, i // 2, body, tuple(_flash_init(REP * tq) for _ in range(N_KV)))
    odd = (i % 2) == 1
    kp = pl.multiple_of(jnp.maximum(i - 1, 0) * MOBA_BLOCK, MOBA_BLOCK)
    ko = pl.multiple_of(i * MOBA_BLOCK, MOBA_BLOCK)
    s_prev = scores(kp, MOBA_BLOCK, qas)
    s_own = scores(ko, MOBA_BLOCK, qos)
    v = jnp.concatenate([v_ref[pl.ds(kp, MOBA_BLOCK), :], v_ref[pl.ds(ko, MOBA_BLOCK), :]], axis=0)
    outs = []
    for g in range(N_KV):
        s = jnp.concatenate([jnp.where(odd, s_prev[g], NEG), jnp.where(cc <= rr, s_own[g], NEG)], axis=1)
        m, l, acc = _flash_step(carry[g], s, v)
        o = acc / l
        outs += [o[r * tq:(r + 1) * tq] for r in range(REP)]
    o_ref[...] = _compact_heads(outs).astype(BF16)


def _moba_prompt(qm, rows, mobak, mobav, nb, seq):
    tq = MOBA_BLOCK
    nq = seq // tq
    n = qm.shape[0]
    full = lambda wd: pl.BlockSpec((seq, wd), lambda b_, i: (b_, 0))
    return pl.pallas_call(
        functools.partial(_moba_kernel, tq=tq, n_blk=nq),
        grid=(nb, nq),
        in_specs=[pl.BlockSpec((tq, QPAD_W), lambda b_, i: (b_ * nq + i, 0)),
                  full(KV_W), full(2 * LANES), full(LANES)],
        out_specs=pl.BlockSpec((tq, N_HEADS * HEAD_DIM), lambda b_, i: (b_ * nq + i, 0)),
        out_shape=jax.ShapeDtypeStruct((n, N_HEADS * HEAD_DIM), BF16),
        scratch_shapes=[pltpu.VMEM((nq, LANES), F32)],
        compiler_params=_params(2),
        name="moba_prompt",
    )(qm, rows, mobak, mobav)


FFN_CHUNK = 256
FFN_TM = 1024


def _ffn_kernel(x_ref, conv_ref, nsa_ref, moba_ref, wo_ref, g_ref, wgu_ref, wd_ref, o_ref):
    mix = jnp.concatenate([conv_ref[...], nsa_ref[...], moba_ref[...]], axis=1)
    x1 = x_ref[...] + _dot(mix, wo_ref[...])
    h2 = _rms(x1, g_ref[...]).astype(BF16)
    acc = x1
    for c0 in range(0, FFN_HIDDEN, FFN_CHUNK):
        gt = _dot(h2, wgu_ref[:, c0:c0 + FFN_CHUNK])
        up = _dot(h2, wgu_ref[:, FFN_HIDDEN + c0:FFN_HIDDEN + c0 + FFN_CHUNK])
        acc = acc + _dot((gt * jax.nn.sigmoid(gt) * up).astype(BF16), wd_ref[c0:c0 + FFN_CHUNK, :])
    o_ref[...] = acc


def _ffn(x, conv_o, nsa_o, moba_o, wo, g, wgu, wd, tm):
    n = x.shape[0]
    row = lambda wd_: pl.BlockSpec((tm, wd_), lambda i: (i, 0))
    return pl.pallas_call(
        _ffn_kernel,
        grid=(n // tm,),
        in_specs=[row(D_MODEL), row(CONV_CH), row(N_HEADS * HEAD_DIM), row(N_HEADS * HEAD_DIM),
                  _const_spec((D_MODEL, D_MODEL)), _const_spec((1, D_MODEL)),
                  _const_spec((D_MODEL, 2 * FFN_HIDDEN)), _const_spec((FFN_HIDDEN, D_MODEL))],
        out_specs=row(D_MODEL),
        out_shape=jax.ShapeDtypeStruct((n, D_MODEL), F32),
        compiler_params=_params(1),
        name="mix_ffn",
    )(x, conv_o, nsa_o, moba_o, wo, g, wgu, wd)


def _norm_kernel(x_ref, g_ref, o_ref):
    o_ref[...] = _rms(x_ref[...], g_ref[...])


def _final_norm(x, g, tm):
    n = x.shape[0]
    return pl.pallas_call(
        _norm_kernel,
        grid=(n // tm,),
        in_specs=[pl.BlockSpec((tm, D_MODEL), lambda i: (i, 0)), _const_spec((1, D_MODEL))],
        out_specs=pl.BlockSpec((tm, D_MODEL), lambda i: (i, 0)),
        out_shape=jax.ShapeDtypeStruct((n, D_MODEL), F32),
        compiler_params=_params(1),
        name="final_norm",
    )(x, g)


def _q8(q, g):
    row = _iota((8, LANES), 0)
    out = jnp.zeros((8, LANES), F32)
    for r in range(REP):
        hh = g * REP + r
        out = jnp.where(row == r, q[:, hh * LANES:(hh + 1) * LANES], out)
    return pltpu.roll(out, HEAD_DIM, 1) if g == 1 else out


def _gather_pages(page_src, page_dst, sem, n_pages, start):
    def body(p, _):
        cp = pltpu.make_async_copy(page_src(p), page_dst(p), sem)
        if start:
            cp.start()
        else:
            cp.wait()
        return 0
    lax.fori_loop(0, n_pages, body, 0)


def _paged_prologue(src_of, dst_of, sem, n_pages):
    b = pl.program_id(0)
    slot = b % 2

    def run(bb, sl, start):
        _gather_pages(lambda p: src_of(bb, p), lambda p: dst_of(sl, p), sem.at[sl], n_pages, start)

    @pl.when(b == 0)
    def _():
        run(0, 0, True)

    run(b, slot, False)

    @pl.when(b + 1 < pl.num_programs(0))
    def _():
        run(b + 1, 1 - slot, True)

    return slot


def _rank_lanes(score):
    n = score.shape[1]
    s_row = jnp.broadcast_to(score, (n, n))
    s_col = s_row.T
    ii = _iota((n, n), 0)
    jj = _iota((n, n), 1)
    beats = (s_col > s_row) | ((s_col == s_row) & (ii < jj))
    return jnp.sum(jnp.where(beats, 1, 0), axis=0, keepdims=True)


def _cmp_sample_kernel(pt_ref, cache_ref, new_ref, q_ref, wbig_ref, w2_ref, pe_ref, mw_ref, perm_ref,
                       o_ref, idx_ref, x_ref, xb_ref, ab_ref, sem, *, layer, n_pages, past):
    slot = _paged_prologue(lambda b, p: cache_ref.at[layer, pt_ref[b, p]], lambda sl, p: x_ref.at[sl, p], sem, n_pages)
    per_page = PAGE // CMP_STRIDE

    def pair_to_chunks(pp):
        r0 = pl.multiple_of(pp * 2 * per_page, 2 * per_page)
        tr = []
        for t in range(2):
            xp = _dot(x_ref[slot, 2 * pp + t].astype(BF16), perm_ref[...])
            tr.append([xp[hv * LANES:(hv + 1) * LANES, :].T for hv in range(2)])
        for i in range(CMP_STRIDE):
            for hv in range(2):
                piece = jnp.concatenate([tr[t][hv][i * per_page:(i + 1) * per_page, :] for t in range(2)], axis=0)
                xb_ref[hv, pl.ds(r0, 2 * per_page), i * LANES:(i + 1) * LANES] = piece.astype(BF16)

    def to_chunks(t, _):
        for u in range(CMP_PAIRS_PER_TRIP):
            pair_to_chunks(t * CMP_PAIRS_PER_TRIP + u)
        return 0

    lax.fori_loop(0, n_pages // (2 * CMP_PAIRS_PER_TRIP), to_chunks, 0)
    n_tok = n_pages * per_page
    for hv in range(2):
        new_chunk = jnp.concatenate([new_ref[0][:, hv * LANES:(hv + 1) * LANES],
                                     jnp.zeros((1, HALF_CHUNK_W - LANES), F32)], axis=1)
        xb_ref[hv, pl.ds(n_tok, CMP_TAIL), :] = jnp.where(_iota((CMP_TAIL, 1), 0) == 0, new_chunk, 0.0).astype(BF16)
    comp = _compress(xb_ref, n_tok, wbig_ref, w2_ref, pe_ref, ab_ref)
    kc = comp[:, :LANES]
    vc = comp[:, LANES:].astype(BF16)
    q = q_ref[0]
    cmask = (_iota((1, n_tok), 1) * CMP_STRIDE + (2 * CMP_STRIDE - 1)) <= past
    row8 = _iota((8, 1), 0)
    n_sb = past // SEL_BLOCK + 1
    cur = past // SEL_BLOCK
    n_l = mw_ref.shape[1]
    jl = _iota((1, n_l), 1)
    for g in range(N_KV):
        p = _softmax_rows(_dot_nt(_q8(q, g), kc, HI), cmask)
        o_ref[0, g] = _dot(p.astype(BF16), vc)
        imp = jnp.sum(jnp.where(row8 < REP, p, 0.0), axis=0, keepdims=True)
        pslc = _dot(jnp.broadcast_to(imp, (8, n_tok)), mw_ref[...], HI)[0:1, :]
        forced = (jl == 0) | (jl == cur) | (jl == cur - 1)
        score = jnp.where(jl >= n_sb, -jnp.inf, jnp.where(forced, BIG, jnp.where(jl <= cur, pslc, NEG)))
        rank = _rank_lanes(score)
        kk = _iota((N_SEL, n_l), 0)
        idx = jnp.sum(jnp.where(rank == kk, _iota((N_SEL, n_l), 1), 0), axis=1, keepdims=True)
        idx_ref[0, g] = jnp.broadcast_to(idx, (N_SEL, LANES))


def _cmp_sample(layer, pt, cache_t, new, q, wbig, w2, pe, mw, past):
    bd, n_pages = pt.shape
    n_tok = n_pages * (PAGE // CMP_STRIDE)
    blk = lambda *s: pl.BlockSpec((1,) + s, lambda b_, pt_: (b_,) + (0,) * len(s))
    cst = lambda shape: pl.BlockSpec(shape, lambda b_, pt_: (0,) * len(shape), pipeline_mode=pl.Buffered(1))
    gs = pltpu.PrefetchScalarGridSpec(
        num_scalar_prefetch=1, grid=(bd,),
        in_specs=[pl.BlockSpec(memory_space=pl.ANY), blk(1, KV_W), blk(1, QPAD_W),
                  cst((2, HALF_CHUNK_W, CMP_AB_W)), cst((4 * CMP_HIDDEN, KV_W)), cst((2, 8, HALF_CHUNK_W)),
                  cst(mw.shape), cst((PAGE, PAGE))],
        out_specs=[blk(N_KV, 8, LANES), blk(N_KV, N_SEL, LANES)],
        scratch_shapes=[pltpu.VMEM((2, n_pages, KV_W, PAGE), F32),
                        pltpu.VMEM((2, n_tok + CMP_TAIL, HALF_CHUNK_W), BF16),
                        pltpu.VMEM((2, n_tok + CMP_TAIL, CMP_AB_W), F32), pltpu.SemaphoreType.DMA((2,))])
    return pl.pallas_call(
        functools.partial(_cmp_sample_kernel, layer=layer, n_pages=n_pages, past=past),
        grid_spec=gs,
        out_shape=[jax.ShapeDtypeStruct((bd, N_KV, 8, LANES), F32), jax.ShapeDtypeStruct((bd, N_KV, N_SEL, LANES), I32)],
        compiler_params=_params(1, 60 * 1024 * 1024),
        name="cmp_sample",
    )(pt, cache_t, new, q, wbig, w2, pe, mw, jnp.asarray(_chunk_perm(), BF16))


def _chunk_perm():
    per_page = PAGE // CMP_STRIDE
    src = np.arange(PAGE)
    dst = (src % CMP_STRIDE) * per_page + src // CMP_STRIDE
    m = np.zeros((PAGE, PAGE), np.float32)
    m[src, dst] = 1.0
    return m


def _one_query_attention(q8, tiles, k_new, v_new):
    qb = q8.astype(BF16)
    ss = []
    for k_t, _, mask in tiles:
        s = _dot(qb, k_t)
        ss.append(s if mask is None else jnp.where(mask, s, NEG))
    s_new = jnp.sum(q8 * k_new, axis=-1, keepdims=True)
    m = functools.reduce(jnp.maximum, [jnp.max(s, axis=-1, keepdims=True) for s in ss] + [s_new])
    es = [jnp.exp(s - m) for s in ss]
    e_new = jnp.exp(s_new - m)
    l = functools.reduce(jnp.add, [jnp.sum(e, axis=-1, keepdims=True) for e in es]) + e_new
    acc = functools.reduce(jnp.add, [_dot_nt(e.astype(BF16), t[1]) for e, t in zip(es, tiles)])
    return (acc + e_new * v_new) / l


def _selwin_sample_kernel(pt_ref, idx_ref, cache_ref, newsel_ref, win_ref, newwin_ref, q_ref,
                          osel_ref, owin_ref, buf_ref, sem, *, layer, n_cb):
    b = pl.program_id(0)
    per_page = PAGE // SEL_BLOCK
    slot = b % 2

    def gather(bb, sl, start):
        for g in range(N_KV):
            for k in range(N_SEL):
                blk = jnp.minimum(idx_ref[bb, g * N_SEL + k], n_cb - 1)
                cp = pltpu.make_async_copy(cache_ref.at[layer, pt_ref[bb, blk // per_page]], buf_ref.at[sl, g, k],
                                           sem.at[sl])
                if start:
                    cp.start()
                else:
                    cp.wait()

    @pl.when(b == 0)
    def _():
        gather(0, 0, True)

    @pl.when(b + 1 < pl.num_programs(0))
    def _():
        gather(b + 1, 1 - slot, True)

    q = q_ref[0]
    new_sel = newsel_ref[0]
    new_win = newwin_ref[0]
    win_k = win_ref[0, 0, :LANES, :].astype(BF16)
    win_v = win_ref[0, 0, LANES:, :].astype(BF16)
    for g in range(N_KV):
        owin_ref[0, g] = _one_query_attention(_q8(q, g), [(win_k, win_v, None)], new_win[:, :LANES], new_win[:, LANES:])
    gather(b, slot, False)
    lane_blk = _iota((1, PAGE), 1) // SEL_BLOCK
    for g in range(N_KV):
        tiles = []
        for k in range(N_SEL):
            blk = idx_ref[b, g * N_SEL + k]
            mask = (lane_blk == blk % per_page) & (blk < n_cb)
            tiles.append((buf_ref[slot, g, k, :LANES, :].astype(BF16), buf_ref[slot, g, k, LANES:, :].astype(BF16),
                          mask))
        osel_ref[0, g] = _one_query_attention(_q8(q, g), tiles, new_sel[:, :LANES], new_sel[:, LANES:])


def _selwin_sample(layer, pt, idx, cache_t, newsel, win_t, newwin, q, past):
    bd = pt.shape[0]
    wb = win_t.shape[-1]
    blk = lambda *s: pl.BlockSpec((1,) + s, lambda b_, *_: (b_,) + (0,) * len(s))
    gs = pltpu.PrefetchScalarGridSpec(
        num_scalar_prefetch=2, grid=(bd,),
        in_specs=[pl.BlockSpec(memory_space=pl.ANY), blk(1, KV_W),
                  pl.BlockSpec((1, 1, KV_W, wb), lambda b_, *_: (layer, b_, 0, 0)), blk(1, KV_W), blk(1, QPAD_W)],
        out_specs=[blk(N_KV, 8, LANES), blk(N_KV, 8, LANES)],
        scratch_shapes=[pltpu.VMEM((2, N_KV, N_SEL, KV_W, PAGE), F32), pltpu.SemaphoreType.DMA((2,))])
    return pl.pallas_call(
        functools.partial(_selwin_sample_kernel, layer=layer, n_cb=past // SEL_BLOCK),
        grid_spec=gs,
        out_shape=[jax.ShapeDtypeStruct((bd, N_KV, 8, LANES), F32)] * 2,
        compiler_params=_params(1),
        name="selwin_sample",
    )(pt, idx, cache_t, newsel, win_t, newwin, q)


MOBA_CHUNK_PAGES = 8


def _moba_sample_kernel(pt_ref, cache_ref, new_ref, q_ref, o_ref, x_ref, s_ref, sem, *, layer, n_pages, n_blk):
    slot = _paged_prologue(lambda b, p: cache_ref.at[layer, pt_ref[b, p]], lambda sl, p: x_ref.at[sl, p], sem, n_pages)
    q = q_ref[0]
    new = new_ref[0]
    per_blk = MOBA_BLOCK // PAGE
    lane_sq = _iota((LANES, LANES), 1)
    km_t = jnp.zeros((LANES, LANES), F32)
    for j in range(n_blk):
        blk = functools.reduce(jnp.add, [x_ref[slot, j * per_blk + t, :LANES, :] for t in range(per_blk)])
        km_t = jnp.where(lane_sq == j, jnp.sum(blk, axis=1, keepdims=True) * (1.0 / MOBA_BLOCK), km_t)
    jl = _iota((1, LANES), 1)
    row8 = _iota((8, LANES), 0)
    q8s, notsels = [], []
    for g in range(N_KV):
        q8 = _q8(q, g)
        gate = _dot(q8, km_t, HI)
        notsel = jnp.ones((8, LANES), F32)
        for r in range(REP):
            score = jnp.where(jl < n_blk, gate[r:r + 1, :], -jnp.inf)
            sel = (_rank_lanes(score) < MOBA_TOPK) & (jl < n_blk)
            notsel = jnp.where((row8 == r) & sel, 0.0, notsel)
        q8s.append(q8)
        notsels.append(notsel)
    cw = MOBA_CHUNK_PAGES * PAGE
    blk_per_chunk = cw // MOBA_BLOCK
    blk_of_lane = _iota((1, cw), 1) // MOBA_BLOCK
    n_chunks = n_pages // MOBA_CHUNK_PAGES
    for c in range(n_chunks):
        k_t = jnp.concatenate([x_ref[slot, c * MOBA_CHUNK_PAGES + t, :LANES, :] for t in range(MOBA_CHUNK_PAGES)],
                              axis=1).astype(BF16)
        for g in range(N_KV):
            off = jnp.zeros((8, cw), F32)
            for t in range(blk_per_chunk):
                j = c * blk_per_chunk + t
                off = jnp.where(blk_of_lane == t, notsels[g][:, j:j + 1], off)
            s_ref[g, :, c * cw:(c + 1) * cw] = jnp.where(off > 0.5, NEG, _dot(q8s[g].astype(BF16), k_t))
    ls, e_news = [], []
    for g in range(N_KV):
        s = s_ref[g]
        s_new = jnp.sum(q8s[g] * new[:, :LANES], axis=-1, keepdims=True)
        m = jnp.maximum(jnp.max(s, axis=-1, keepdims=True), s_new)
        e = jnp.exp(s - m)
        e_new = jnp.exp(s_new - m)
        ls.append(jnp.sum(e, axis=-1, keepdims=True) + e_new)
        e_news.append(e_new)
        s_ref[g] = e
    acc = [e_news[g] * new[:, LANES:] for g in range(N_KV)]
    for c in range(n_chunks):
        v_t = jnp.concatenate([x_ref[slot, c * MOBA_CHUNK_PAGES + t, LANES:, :] for t in range(MOBA_CHUNK_PAGES)],
                              axis=1).astype(BF16)
        for g in range(N_KV):
            acc[g] = acc[g] + _dot_nt(s_ref[g, :, c * cw:(c + 1) * cw].astype(BF16), v_t)
    for g in range(N_KV):
        o_ref[0, g] = acc[g] / ls[g]


def _moba_sample(layer, pt, cache_t, new, q, past):
    bd, n_pages = pt.shape
    n_blk = past // MOBA_BLOCK
    blk = lambda *s: pl.BlockSpec((1,) + s, lambda b_, pt_: (b_,) + (0,) * len(s))
    gs = pltpu.PrefetchScalarGridSpec(
        num_scalar_prefetch=1, grid=(bd,),
        in_specs=[pl.BlockSpec(memory_space=pl.ANY), blk(1, KV_W), blk(1, QPAD_W)],
        out_specs=blk(N_KV, 8, LANES),
        scratch_shapes=[pltpu.VMEM((2, n_pages, KV_W, PAGE), F32), pltpu.VMEM((N_KV, 8, past), F32),
                        pltpu.SemaphoreType.DMA((2,))])
    return pl.pallas_call(
        functools.partial(_moba_sample_kernel, layer=layer, n_pages=n_pages, n_blk=n_blk),
        grid_spec=gs,
        out_shape=jax.ShapeDtypeStruct((bd, N_KV, 8, LANES), F32),
        compiler_params=_params(1),
        name="moba_sample",
    )(pt, cache_t, new, q)


def _combine_sample_kernel(u_ref, st_ref, w_ref, b_ref, lg_ref, lb_ref, gates_ref, ocmp_ref, osel_ref, owin_ref,
                           omoba_ref, conv_ref, nsa_ref, moba_ref):
    y = u_ref[...] * w_ref[CONV_WIDTH - 1:CONV_WIDTH, :]
    for j in range(CONV_WIDTH - 1):
        y = y + st_ref[j] * w_ref[j:j + 1, :]
    conv_ref[...] = _conv_post(y, b_ref, lg_ref, lb_ref).astype(BF16)
    gates = gates_ref[...]
    w = N_HEADS * HEAD_DIM
    head_of_lane = _iota((1, w), 1) // HEAD_DIM
    acc = jnp.zeros(ocmp_ref.shape, F32)
    for br, ref in enumerate((ocmp_ref, osel_ref, owin_ref)):
        ge = jnp.zeros(ocmp_ref.shape, F32)
        for hh in range(N_HEADS):
            c = br * N_HEADS + hh
            ge = jnp.where(head_of_lane == hh, gates[:, c:c + 1], ge)
        acc = acc + ge * ref[...]
    nsa_ref[...] = acc.astype(BF16)
    moba_ref[...] = omoba_ref[...].astype(BF16)


def _combine_sample(u, st_t, w, b, lg, lb, gates, o_cmp, o_sel, o_win, o_moba):
    bd = u.shape[0]
    w_h = N_HEADS * HEAD_DIM
    return pl.pallas_call(
        _combine_sample_kernel,
        out_shape=[jax.ShapeDtypeStruct((bd, CONV_CH), BF16), jax.ShapeDtypeStruct((bd, w_h), BF16),
                   jax.ShapeDtypeStruct((bd, w_h), BF16)],
        name="combine_sample",
    )(u, st_t, w, b, lg, lb, gates, o_cmp, o_sel, o_win, o_moba)


def _rope_tables(pos):
    half = HEAD_DIM // 2
    inv = ROPE_THETA ** (-jnp.arange(half, dtype=F32) / half)
    ang = pos.astype(F32)[:, None] * inv[None, :]
    cos, sin = jnp.cos(ang), jnp.sin(ang)
    zero = jnp.zeros_like(sin)
    return (jnp.tile(cos, (1, 4)), jnp.tile(jnp.concatenate([-sin, zero], axis=1), (1, 2)),
            jnp.tile(jnp.concatenate([zero, sin], axis=1), (1, 2)))


def _pad_heads(w):
    d = w.shape[:-1]
    w = w.reshape(*d, N_HEADS, HEAD_DIM)
    return jnp.pad(w, [(0, 0)] * len(d) + [(0, 0), (0, LANES - HEAD_DIM)]).reshape(*d, QPAD_W)


def _relayout_w_in(w_in):
    c_q = 2 * CONV_CH
    c_kv = c_q + N_HEADS * HEAD_DIM
    c_g = c_kv + 6 * N_KV * HEAD_DIM
    c_qm = c_g + 3 * N_HEADS
    c_kvm = c_qm + N_HEADS * HEAD_DIM
    gates = jnp.pad(w_in[..., c_g:c_qm], ((0, 0), (0, 0), (0, LANES - 3 * N_HEADS)))
    return jnp.concatenate([w_in[..., :c_q], _pad_heads(w_in[..., c_q:c_kv]), w_in[..., c_kv:c_g],
                            _pad_heads(w_in[..., c_qm:c_kvm]), w_in[..., c_kvm:], gates], axis=-1).astype(BF16)


def _compress_weights(pe_k, pe_v, wk1, wk2, wv1, wv2):
    depth = wk1.shape[0]
    w1 = jnp.stack([wk1, wv1], axis=1).reshape(depth, 2, 2, CMP_STRIDE, HEAD_DIM, CMP_HIDDEN)
    wkv = jnp.einsum('lkaidh,gt->lkigdath', w1, jnp.eye(N_KV, dtype=F32))
    wkv = wkv.reshape(depth, 2, HALF_CHUNK_W, CMP_AB_W).astype(BF16)
    w2 = jnp.stack([wk2, wk2, wv2, wv2], axis=1)
    w2big = jnp.einsum('lshd,st->lshtd', w2, jnp.eye(4, dtype=F32)).reshape(depth, 4 * CMP_HIDDEN, KV_W).astype(BF16)
    pe = jnp.stack([pe_k, pe_v], axis=1).reshape(depth, 2, 2, CMP_STRIDE, 1, HEAD_DIM)
    pe = jnp.broadcast_to(pe, (depth, 2, 2, CMP_STRIDE, N_KV, HEAD_DIM)).reshape(depth, 2, 2, HALF_CHUNK_W)
    return wkv, w2big, jnp.pad(pe, ((0, 0), (0, 0), (0, 6), (0, 0)))


def _slc_matrix(n_c, n_sb):
    ratio = SEL_BLOCK // CMP_STRIDE
    c = np.arange(n_c)[:, None]
    j = np.arange(n_sb)[None, :]
    m = np.where(c == ratio * j, 1.0, 0.0) + np.where((c > ratio * j) & (c < ratio * (j + 1)), 2.0, 0.0) \
        + np.where(c == ratio * (j + 1), 1.0, 0.0)
    return m.astype(np.float32)


def _heads_from_q8(o):
    return jnp.concatenate([o[:, 0, :REP, :HEAD_DIM], o[:, 1, :REP, HEAD_DIM:]], axis=1).reshape(o.shape[0], -1)


def kernel(x_prompt, x_sample, cache_nsa_cmp, cache_nsa_sel, cache_moba, state_nsa_win, state_conv, page_table,
           g_mix, w_in, conv_w, conv_b, conv_ln_g, conv_ln_b, cmp_pe_k, cmp_pe_v, cmp_wk1, cmp_wk2, cmp_wv1, cmp_wv2,
           w_out, g_ffn, w_gate_up, w_down, g_final):
    nb, seq, _ = x_prompt.shape
    bd, dec_seq, _ = x_sample.shape
    depth = g_mix.shape[0]
    n_pages = page_table.shape[1]
    past = n_pages * PAGE
    assert dec_seq == 1 and seq % 512 == 0 and seq // SEL_BLOCK <= HEAD_DIM and 3 * (seq // MOBA_BLOCK) <= HEAD_DIM
    assert past % MOBA_BLOCK == 0 and past // MOBA_BLOCK <= LANES and bd % 8 == 0 and n_pages % MOBA_CHUNK_PAGES == 0
    assert n_pages % (2 * CMP_PAIRS_PER_TRIP) == 0
    n_p = nb * seq
    tm = 512

    w_in_r = _relayout_w_in(w_in)
    wbig, w2big, pe_rows = _compress_weights(cmp_pe_k, cmp_pe_v, cmp_wk1, cmp_wk2, cmp_wv1, cmp_wv2)
    w_out_b, w_gu_b, w_down_b = w_out.astype(BF16), w_gate_up.astype(BF16), w_down.astype(BF16)
    conv_w_p = jnp.pad(conv_w, ((0, 0), (0, CONV_HALO - CONV_WIDTH), (0, 0)))
    tab_p = _rope_tables(jnp.arange(seq))
    tab_s = _rope_tables(jnp.full((bd,), past))
    n_c = seq // CMP_STRIDE
    mwt_p = jnp.asarray(_slc_matrix(n_c, seq // SEL_BLOCK).T)
    n_sb_s = past // SEL_BLOCK + 1
    mw_s = jnp.asarray(np.pad(_slc_matrix(past // CMP_STRIDE, n_sb_s), ((0, 0), (0, -n_sb_s % LANES))))

    feat_major = lambda a: a.transpose(0, 1, 3, 4, 5, 2).reshape(a.shape[0], a.shape[1], KV_W, a.shape[2])
    cmp_t, sel_t, moba_t, win_t = (feat_major(a) for a in (cache_nsa_cmp, cache_nsa_sel, cache_moba, state_nsa_win))

    xp = x_prompt.reshape(n_p, D_MODEL)
    xs = x_sample.reshape(bd, D_MODEL)
    new_p = [[], [], [], [], []]
    new_s = [[], [], [], [], []]
    r2 = lambda a: a.reshape(1, -1)
    for l in range(depth):
        (u, q, qm, gates, cmp_rows, moba_rows, selk, selv, wink, winv, mobak, mobav,
         cmp_n, sel_n, win_n, moba_n) = _proj(xp, r2(g_mix[l]), w_in_r[l], *tab_p, seq, tm, True)
        conv_o = _conv(u, conv_w_p[l], r2(conv_b[l]), r2(conv_ln_g[l]), r2(conv_ln_b[l]), nb, seq, 512)
        comp = _compress_prompt(cmp_rows, wbig[l], w2big[l], pe_rows[l], nb, seq)
        nsa_o = _nsa_prompt(q, gates, comp, selk, selv, wink, winv, mwt_p, nb, seq)
        moba_o = _moba_prompt(qm, moba_rows, mobak, mobav, nb, seq)
        xp = _ffn(xp, conv_o, nsa_o, moba_o, w_out_b[l], r2(g_ffn[l]), w_gu_b[l], w_down_b[l], FFN_TM)
        kv6 = lambda a, n_: a.reshape(n_, -1, 2, N_KV, HEAD_DIM)
        from_feat = lambda a: a.reshape(nb, 2, N_KV, HEAD_DIM, -1).transpose(0, 4, 1, 2, 3)
        new_p[0].append(from_feat(cmp_n))
        new_p[1].append(from_feat(sel_n))
        new_p[2].append(from_feat(moba_n))
        new_p[3].append(from_feat(win_n[:, :, -min(WINDOW, seq):]))
        new_p[4].append(u.reshape(nb, seq, CONV_CH)[:, -(CONV_WIDTH - 1):])

        (u, q, qm, gates, cmp_rows, sel_rows, win_rows, moba_rows) = _proj(xs, r2(g_mix[l]), w_in_r[l], *tab_s, bd, bd,
                                                                           False)
        b3 = lambda a: a.reshape(bd, 1, -1)
        o_cmp, idx = _cmp_sample(l, page_table, cmp_t, b3(cmp_rows), b3(q), wbig[l], w2big[l], pe_rows[l],
                                 mw_s, past)
        o_sel, o_win = _selwin_sample(l, page_table, idx[:, :, :, 0].reshape(bd, N_KV * N_SEL), sel_t,
                                      b3(sel_rows), win_t, b3(win_rows), b3(q), past)
        o_moba = _moba_sample(l, page_table, moba_t, b3(moba_rows), b3(qm), past)
        conv_o, nsa_o, moba_o = _combine_sample(
            u, state_conv[l].transpose(1, 0, 2), conv_w_p[l], r2(conv_b[l]), r2(conv_ln_g[l]), r2(conv_ln_b[l]), gates,
            _heads_from_q8(o_cmp), _heads_from_q8(o_sel), _heads_from_q8(o_win), _heads_from_q8(o_moba))
        xs = _ffn(xs, conv_o, nsa_o, moba_o, w_out_b[l], r2(g_ffn[l]), w_gu_b[l], w_down_b[l], bd)
        new_s[0].append(kv6(cmp_rows, bd))
        new_s[1].append(kv6(sel_rows, bd))
        new_s[2].append(kv6(moba_rows, bd))
        new_s[3].append(jnp.concatenate([state_nsa_win[l], kv6(win_rows, bd)], axis=1)[:, -state_nsa_win.shape[2]:])
        new_s[4].append(jnp.concatenate([state_conv[l], u[:, None, :]], axis=1)[:, -(CONV_WIDTH - 1):])

    y_prompt = _final_norm(xp, r2(g_final), tm).reshape(nb, seq, D_MODEL)
    y_sample = _final_norm(xs, r2(g_final), bd).reshape(bd, 1, D_MODEL)
    return (y_prompt, y_sample,
            jnp.stack(new_p[0]), jnp.stack(new_s[0]),
            jnp.stack(new_p[1]), jnp.stack(new_s[1]),
            jnp.stack(new_p[2]), jnp.stack(new_s[2]),
            jnp.stack(new_p[3]), jnp.stack(new_s[3]),
            jnp.stack(new_p[4]), jnp.stack(new_s[4]))
```

```python
import functools

import numpy as np
import jax
import jax.numpy as jnp
from jax import lax
from jax.experimental import pallas as pl
from jax.experimental.pallas import tpu as pltpu

F32 = jnp.float32
BF16 = jnp.bfloat16
I32 = jnp.int32

D_MODEL = 1024
HEAD_DIM = 64
LANES = 128
CONV_CH = 256
N_HEADS = 6
N_KV = 2
REP = N_HEADS // N_KV
CONV_WIDTH = 31
CMP_STRIDE = 16
CMP_HIDDEN = 128
SEL_BLOCK = 64
N_SEL = 16
WINDOW = 512
BAND = 128
MOBA_BLOCK = 256
MOBA_TOPK = 3
PAGE = 128
FFN_HIDDEN = 2816
ROPE_THETA = 10000.0
EPS = 1e-6
NEG = -1e30
BIG = 1e30
ATTN_SCALE = HEAD_DIM ** -0.5
HI = lax.Precision.HIGHEST

KV_W = 2 * N_KV * HEAD_DIM
QPAD_W = N_HEADS * LANES
CHUNK_W = CMP_STRIDE * KV_W

OFF_U, OFF_Q, OFF_KV, OFF_QM, OFF_KVM, OFF_G, W_IN_COLS = 0, 512, 1280, 2048, 2816, 3072, 3200

VMEM_LIMIT = 56 * 1024 * 1024


def _params(n_axes, limit=VMEM_LIMIT):
    return pltpu.CompilerParams(dimension_semantics=("arbitrary",) * n_axes, vmem_limit_bytes=limit)


def _const_spec(shape):
    nd = len(shape)
    return pl.BlockSpec(shape, lambda *_: (0,) * nd, pipeline_mode=pl.Buffered(1))


def _dot(a, b, precision=None):
    return jnp.dot(a, b, preferred_element_type=F32, precision=precision)


def _dot_nt(a, b, precision=None):
    return lax.dot_general(a, b, (((1,), (1,)), ((), ())), preferred_element_type=F32, precision=precision)


def _iota(shape, dim):
    return lax.broadcasted_iota(I32, shape, dim)


def _rms(x, g):
    return x * lax.rsqrt(jnp.mean(x * x, axis=-1, keepdims=True) + EPS) * g


def _compact_heads(a):
    lo = _iota(a[0].shape, 1) < HEAD_DIM
    return jnp.concatenate([
        jnp.where(lo, a[0], pltpu.roll(a[1], HEAD_DIM, 1)),
        jnp.where(lo, a[2], a[3]),
        jnp.where(lo, pltpu.roll(a[4], HEAD_DIM, 1), a[5])], axis=1)


def _proj_kernel(x_ref, g_ref, w_ref, cos_ref, sa_ref, sb_ref, u_ref, q_ref, qm_ref, gates_ref, *refs,
                 seq, tm, prompt):
    if prompt:
        (cmp_ref, moba_ref, selk_ref, selv_ref, wink_ref, winv_ref, mobak_ref, mobav_ref,
         cmp_t_ref, sel_t_ref, win_t_ref, moba_t_ref) = refs
        sel_ref = win_ref = None
    else:
        cmp_ref, sel_ref, win_ref, moba_ref = refs
        cmp_t_ref = sel_t_ref = win_t_ref = moba_t_ref = None
    i = pl.program_id(0)
    h = _rms(x_ref[...], g_ref[...]).astype(BF16)
    cos, sa, sb = cos_ref[...], sa_ref[...], sb_ref[...]

    def seg(off, width):
        return _dot(h, w_ref[:, off:off + width])

    def rope(z):
        return z * cos + pltpu.roll(z, LANES - 32, 1) * sa + pltpu.roll(z, 32, 1) * sb

    zu = seg(OFF_U, 2 * CONV_CH)
    u_ref[...] = zu[:, :CONV_CH] * jax.nn.sigmoid(zu[:, CONV_CH:])

    for off, ref in ((OFF_Q, q_ref), (OFF_QM, qm_ref)):
        z = seg(off, QPAD_W)
        for hh in range(N_HEADS):
            ref[:, hh * LANES:(hh + 1) * LANES] = rope(z[:, hh * LANES:(hh + 1) * LANES]) * ATTN_SCALE

    t = (i * tm) % seq + _iota((tm, LANES), 0)
    lane = _iota((tm, LANES), 1)
    lo = lane < HEAD_DIM
    oh_sel = jnp.where((lane >= HEAD_DIM) & ((t >> 6) == lane - HEAD_DIM), NEG, 0.0)
    oh_moba = jnp.where((lane >= HEAD_DIM) & (lane < HEAD_DIM + REP * MOBA_LANES)
                        & ((t >> 8) == ((lane - HEAD_DIM) & (MOBA_LANES - 1))), NEG, 0.0)

    def rows(off, ref, t_ref):
        z = seg(off, KV_W)
        k = rope(z[:, :LANES])
        v = z[:, LANES:]
        if ref is not None:
            ref[:, :LANES] = k
            ref[:, LANES:] = v
        if t_ref is not None:
            t_ref[0, :LANES, :] = k.T
            t_ref[0, LANES:, :] = v.T
        return k, v

    rows(OFF_KV, cmp_ref, cmp_t_ref)
    k, v = rows(OFF_KV + KV_W, sel_ref, sel_t_ref)
    if prompt:
        selk_ref[:, :LANES] = jnp.where(lo, k, oh_sel).astype(BF16)
        selk_ref[:, LANES:] = jnp.where(lo, pltpu.roll(k, HEAD_DIM, 1), oh_sel).astype(BF16)
        selv_ref[:, :LANES] = jnp.where(lo, v, 1.0).astype(BF16)
        selv_ref[:, LANES:] = jnp.where(lo, 1.0, v).astype(BF16)
    k, v = rows(OFF_KV + 2 * KV_W, win_ref, win_t_ref)
    if prompt:
        wink_ref[:, :LANES] = k.astype(BF16)
        wink_ref[:, LANES:] = pltpu.roll(k, HEAD_DIM, 1).astype(BF16)
        winv_ref[...] = v.astype(BF16)
    k, v = rows(OFF_KVM, moba_ref, moba_t_ref)
    if prompt:
        mobak_ref[:, :LANES] = jnp.where(lo, k, oh_moba).astype(BF16)
        mobak_ref[:, LANES:] = jnp.where(lo, pltpu.roll(k, HEAD_DIM, 1), oh_moba).astype(BF16)
        mobav_ref[:, :LANES] = jnp.where(lo, v, 1.0).astype(BF16)
        mobav_ref[:, LANES:] = jnp.where(lo, 1.0, v).astype(BF16)

    gates_ref[...] = jax.nn.sigmoid(seg(OFF_G, LANES))


def _proj(x, g, w, cos, sa, sb, seq, tm, prompt):
    n = x.shape[0]
    nt = cos.shape[0] // tm
    row = lambda wd: pl.BlockSpec((tm, wd), lambda i: (i, 0))
    tab = pl.BlockSpec((tm, LANES), lambda i: (i % nt, 0))
    shapes = [((n, wd), F32) for wd in (CONV_CH, QPAD_W, QPAD_W, LANES)]
    specs = [row(s[1]) for s, _ in shapes]
    if prompt:
        extra = [((n, KV_W), F32)] * 2 + [((n, wd), BF16) for wd in (2 * LANES, 2 * LANES, 2 * LANES, LANES,
                                                                     2 * LANES, 2 * LANES)]
        shapes += extra + [((n // seq, KV_W, seq), F32)] * 4
        specs += [row(s[1]) for s, _ in extra]
        specs += [pl.BlockSpec((1, KV_W, tm), lambda i: (i // nt, 0, i % nt))] * 4
    else:
        shapes += [((n, KV_W), F32)] * 4
        specs += [row(KV_W)] * 4
    return pl.pallas_call(
        functools.partial(_proj_kernel, seq=seq, tm=tm, prompt=prompt),
        grid=(n // tm,),
        in_specs=[row(D_MODEL), _const_spec((1, D_MODEL)), _const_spec((D_MODEL, W_IN_COLS)), tab, tab, tab],
        out_specs=specs,
        out_shape=[jax.ShapeDtypeStruct(s, d) for s, d in shapes],
        compiler_params=_params(1),
        name="proj",
    )(x, g, w, cos, sa, sb)


CONV_HALO = 32


def _conv_post(y, b_ref, lg_ref, lb_ref):
    y = y + b_ref[...]
    mu = jnp.mean(y, axis=-1, keepdims=True)
    var = jnp.mean(jnp.square(y - mu), axis=-1, keepdims=True)
    y = (y - mu) * lax.rsqrt(var + EPS) * lg_ref[...] + lb_ref[...]
    return y * jax.nn.sigmoid(y)


def _conv_kernel(u_ref, prev_ref, w_ref, b_ref, lg_ref, lb_ref, o_ref, ext_ref, *, tq):
    i = pl.program_id(1)
    ext_ref[:CONV_HALO, :] = jnp.where(i > 0, prev_ref[...], 0.0)
    ext_ref[CONV_HALO:, :] = u_ref[...]
    off = CONV_HALO - (CONV_WIDTH - 1)
    y = jnp.zeros((tq, CONV_CH), F32)
    for j in range(CONV_WIDTH):
        y = y + ext_ref[pl.ds(off + j, tq), :] * w_ref[j:j + 1, :]
    o_ref[...] = _conv_post(y, b_ref, lg_ref, lb_ref).astype(BF16)


def _conv(u, w, b, lg, lb, nb, seq, tq):
    n = u.shape[0]
    nq = seq // tq
    per = tq // CONV_HALO
    return pl.pallas_call(
        functools.partial(_conv_kernel, tq=tq),
        grid=(nb, nq),
        in_specs=[pl.BlockSpec((tq, CONV_CH), lambda b_, i: (b_ * nq + i, 0)),
                  pl.BlockSpec((CONV_HALO, CONV_CH), lambda b_, i: (jnp.maximum((b_ * nq + i) * per - 1, 0), 0)),
                  _const_spec((CONV_HALO, CONV_CH)), _const_spec((1, CONV_CH)), _const_spec((1, CONV_CH)),
                  _const_spec((1, CONV_CH))],
        out_specs=pl.BlockSpec((tq, CONV_CH), lambda b_, i: (b_ * nq + i, 0)),
        out_shape=jax.ShapeDtypeStruct((n, CONV_CH), BF16),
        scratch_shapes=[pltpu.VMEM((tq + CONV_HALO, CONV_CH), F32)],
        compiler_params=_params(2),
        name="conv",
    )(u, u, w, b, lg, lb)


def _gelu_tanh(x):
    return x * (0.5 * (1.0 + jnp.tanh(0.7978845608028654 * (x + 0.044715 * (x * x * x)))))


CMP_TAIL = 16
CMP_PAIRS_PER_TRIP = 4


HALF_CHUNK_W = CHUNK_W // 2
CMP_AB_W = 4 * CMP_HIDDEN


def _compress(xb_ref, n_tok, wkv_ref, w2_ref, pe_ref, ab_ref):
    half = CMP_AB_W // 2
    hid = []
    for kv in range(2):
        ab_ref[kv] = _dot(xb_ref[kv], wkv_ref[kv])
        pw = _dot(pe_ref[kv].astype(BF16), wkv_ref[kv])
        pe_all = pw[0:1, :half] + pw[1:2, half:]
        hid.append(ab_ref[kv, pl.ds(0, n_tok), :half] + ab_ref[kv, pl.ds(1, n_tok), half:] + pe_all)
    return _dot(_gelu_tanh(jnp.concatenate(hid, axis=1)).astype(BF16), w2_ref[...])


def _compress_kernel(x_ref, wkv_ref, w2_ref, pe_ref, o_ref, x_scr, ab_ref, *, n_tok):
    x = x_ref[0]
    for kv in range(2):
        x_scr[kv, pl.ds(0, n_tok), :] = jnp.concatenate(
            [x[:, i * KV_W + kv * LANES:i * KV_W + (kv + 1) * LANES] for i in range(CMP_STRIDE)], axis=1).astype(BF16)
        x_scr[kv, pl.ds(n_tok, CMP_TAIL), :] = jnp.zeros((CMP_TAIL, HALF_CHUNK_W), BF16)
    o_ref[0] = _compress(x_scr, n_tok, wkv_ref, w2_ref, pe_ref, ab_ref)


def _compress_prompt(rows, wkv, w2, pe, nb, seq):
    n_tok = seq // CMP_STRIDE
    x = rows.reshape(nb, n_tok, CHUNK_W)
    return pl.pallas_call(
        functools.partial(_compress_kernel, n_tok=n_tok),
        grid=(nb,),
        in_specs=[pl.BlockSpec((1, n_tok, CHUNK_W), lambda b_: (b_, 0, 0)),
                  _const_spec((2, HALF_CHUNK_W, CMP_AB_W)), _const_spec((4 * CMP_HIDDEN, KV_W)),
                  _const_spec((2, 8, HALF_CHUNK_W))],
        out_specs=pl.BlockSpec((1, n_tok, KV_W), lambda b_: (b_, 0, 0)),
        out_shape=jax.ShapeDtypeStruct((nb, n_tok, KV_W), F32),
        scratch_shapes=[pltpu.VMEM((2, n_tok + CMP_TAIL, HALF_CHUNK_W), BF16),
                        pltpu.VMEM((2, n_tok + CMP_TAIL, CMP_AB_W), F32)],
        compiler_params=_params(1),
        name="compress_prompt",
    )(x, wkv, w2, pe)


def _softmax_rows(s, mask):
    s = jnp.where(mask, s, NEG)
    m = jnp.max(s, axis=-1, keepdims=True)
    e = jnp.where(mask, jnp.exp(s - m), 0.0)
    l = jnp.sum(e, axis=-1, keepdims=True)
    return e / jnp.where(l > 0.0, l, 1.0)


def _rank_rows(score, n):
    sub = 8
    tiles = [score[sub * r:sub * (r + 1), :] for r in range(n // sub)]
    ranks = [jnp.zeros(t.shape, I32) for t in tiles]
    jj = _iota(tiles[0].shape, 0)
    for ii in range(n):
        row = score[ii:ii + 1, :]
        for r, t in enumerate(tiles):
            if sub * r + sub - 1 < ii:
                beats = row > t
            elif sub * r > ii:
                beats = row >= t
            else:
                beats = (row > t) | ((row == t) & (ii < sub * r + jj))
            ranks[r] = ranks[r] + jnp.where(beats, 1, 0)
    return jnp.concatenate(ranks, axis=0)


def _flash_step(carry, s, v, g):
    m, c = carry
    m_new = jnp.maximum(m, jnp.max(s, axis=-1, keepdims=True))
    p = jnp.exp(s - m_new)
    return m_new, jnp.exp(m - m_new) * c + _dot(p.astype(BF16), v)


def _flash_init(rows, g):
    return jnp.full((rows, 1), NEG, F32), jnp.zeros((rows, LANES), F32)


def _flash_out(carry, g):
    c = carry[1]
    ll = (1 - g) * HEAD_DIM
    return c / c[:, ll:ll + 1]


NSA_TQ = 256
SEL_TK = 512


def _nsa_kernel(q_ref, gates_ref, comp_ref, selk_ref, selv_ref, wink_ref, winv_ref, mwt_ref, o_ref, *, tq):
    i = pl.program_id(1)
    t0 = i * tq
    q = q_ref[...]
    comp = comp_ref[0]
    n_c = comp.shape[0]
    kc = comp[:, :LANES]
    kc_g = (kc, pltpu.roll(kc, HEAD_DIM, 1))
    vc = comp[:, LANES:].astype(BF16)
    pos_col = t0 + _iota((tq, 1), 0)
    cmask = (_iota((1, n_c), 1) * CMP_STRIDE + (2 * CMP_STRIDE - 1)) <= pos_col
    n_sb = mwt_ref.shape[0]

    def head(hh):
        return q[:, hh * LANES:(hh + 1) * LANES]

    o_cmp, ns = [], []
    for g in range(N_KV):
        imp = jnp.zeros((tq, n_c), F32)
        for r in range(REP):
            p = _softmax_rows(_dot_nt(head(g * REP + r), kc_g[g], HI), cmask)
            imp = imp + p
            o_cmp.append(_dot(p.astype(BF16), vc))
        pslc_t = _dot_nt(mwt_ref[...], imp, HI)
        jj = _iota((n_sb, tq), 0)
        cur = (t0 + _iota((n_sb, tq), 1)) >> 6
        forced = (jj == 0) | (jj == cur) | (jj == cur - 1)
        score = jnp.where(forced, BIG, jnp.where(jj <= cur, pslc_t, NEG))
        notsel = jnp.where(_rank_rows(score, n_sb) < N_SEL, 0.0, 1.0)
        parts = [jnp.zeros((HEAD_DIM, tq), F32), notsel]
        if n_sb < HEAD_DIM:
            parts.append(jnp.zeros((HEAD_DIM - n_sb, tq), F32))
        ns.append(jnp.concatenate(parts, axis=0).T.astype(BF16))

    row_pos = t0 + (_iota((REP * tq, 1), 0) % tq)

    qas = [jnp.concatenate([head(g * REP + r).astype(BF16) + ns[g] for r in range(REP)], axis=0) for g in range(N_KV)]

    def tile(kt):
        k0 = pl.multiple_of(kt * SEL_TK, SEL_TK)
        v = [selv_ref[pl.ds(k0, SEL_TK), g * LANES:(g + 1) * LANES] for g in range(N_KV)]
        return [_dot_nt(qas[g], selk_ref[pl.ds(k0, SEL_TK), g * LANES:(g + 1) * LANES]) for g in range(N_KV)], v

    def body(kt, carry):
        ss, v = tile(kt)
        return tuple(_flash_step(carry[g], ss[g], v[g], g) for g in range(N_KV))

    last = t0 // SEL_TK
    carry = lax.fori_loop(0, last, body, tuple(_flash_init(REP * tq, g) for g in range(N_KV)))
    ss, v = tile(last)
    kpos = last * SEL_TK + _iota((1, SEL_TK), 1)
    o_sel = []
    for g in range(N_KV):
        o = _flash_out(_flash_step(carry[g], jnp.where(kpos <= row_pos, ss[g], NEG), v[g], g), g)
        o_sel += [o[r * tq:(r + 1) * tq] for r in range(REP)]

    o_win = []
    n_w = (WINDOW + tq) // BAND
    cc = _iota((1, BAND), 1)
    rr = _iota((REP * tq, 1), 0) % tq
    for g in range(N_KV):
        qw = jnp.concatenate([head(g * REP + r).astype(BF16) for r in range(REP)], axis=0)
        ss, vs = [], []
        for j in range(n_w):
            k0 = t0 - WINDOW + j * BAND
            ok = k0 >= 0
            k0c = pl.multiple_of(jnp.maximum(k0, 0), BAND)
            s = _dot_nt(qw, wink_ref[pl.ds(k0c, BAND), g * LANES:(g + 1) * LANES])
            if BAND * j < tq:
                ok = ok & (cc >= rr - BAND * j)
            if WINDOW - BAND * j < BAND:
                ok = ok & (cc <= rr + (WINDOW - BAND * j))
            ss.append(jnp.where(ok, s, NEG))
            vs.append(winv_ref[pl.ds(k0c, BAND), :])
        m = functools.reduce(jnp.maximum, [jnp.max(s, axis=-1, keepdims=True) for s in ss])
        es = [jnp.exp(s - m) for s in ss]
        l = functools.reduce(jnp.add, [jnp.sum(e, axis=-1, keepdims=True) for e in es])
        acc = functools.reduce(jnp.add, [_dot(e.astype(BF16), v) for e, v in zip(es, vs)])
        o = acc / l
        o_win += [o[r * tq:(r + 1) * tq] for r in range(REP)]

    gates = gates_ref[...]
    mixed = []
    for hh in range(N_HEADS):
        gc = [gates[:, br * N_HEADS + hh:br * N_HEADS + hh + 1] for br in range(3)]
        mixed.append(gc[0] * o_cmp[hh] + gc[1] * o_sel[hh] + gc[2] * o_win[hh])
    o_ref[...] = _compact_heads(mixed).astype(BF16)


def _nsa_prompt(q, gates, comp, selk, selv, wink, winv, mwt, nb, seq):
    tq = NSA_TQ
    nq = seq // tq
    n = q.shape[0]
    n_c = comp.shape[1]
    row = lambda wd: pl.BlockSpec((tq, wd), lambda b_, i: (b_ * nq + i, 0))
    full = lambda wd: pl.BlockSpec((seq, wd), lambda b_, i: (b_, 0))
    return pl.pallas_call(
        functools.partial(_nsa_kernel, tq=tq),
        grid=(nb, nq),
        in_specs=[row(QPAD_W), row(LANES), pl.BlockSpec((1, n_c, KV_W), lambda b_, i: (b_, 0, 0)),
                  full(2 * LANES), full(2 * LANES), full(2 * LANES), full(LANES), _const_spec(mwt.shape)],
        out_specs=row(N_HEADS * HEAD_DIM),
        out_shape=jax.ShapeDtypeStruct((n, N_HEADS * HEAD_DIM), BF16),
        compiler_params=_params(2),
        name="nsa_prompt",
    )(q, gates, comp, selk, selv, wink, winv, mwt)


MOBA_LANES = 16


def _moba_kernel(q_ref, rows_ref, k_ref, v_ref, o_ref, kmean_ref, *, tq, n_blk):
    i = pl.program_id(1)

    @pl.when(i == 0)
    def _():
        for j in range(n_blk):
            kmean_ref[j:j + 1, :] = jnp.sum(rows_ref[j * MOBA_BLOCK:(j + 1) * MOBA_BLOCK, :LANES],
                                            axis=0, keepdims=True) * (1.0 / MOBA_BLOCK)

    q = q_ref[...]
    km = kmean_ref[...]
    km_g = (km, pltpu.roll(km, HEAD_DIM, 1))
    jj = _iota((n_blk, tq), 0)
    lane = _iota((tq, LANES), 1)
    cc = _iota((1, MOBA_BLOCK), 1)
    rr = _iota((REP * tq, 1), 0) % tq

    def head(hh):
        return q[:, hh * LANES:(hh + 1) * LANES]

    qas, qos = [], []
    for g in range(N_KV):
        ns_rows = [jnp.zeros((HEAD_DIM, tq), F32)]
        for r in range(REP):
            gate = jnp.where(jj < i, _dot_nt(km_g[g], head(g * REP + r), HI), NEG)
            sel = (_rank_rows(gate, n_blk) < MOBA_TOPK) & (jj < i)
            ns_rows.append(jnp.where(sel, 0.0, 1.0))
            if n_blk < MOBA_LANES:
                ns_rows.append(jnp.zeros((MOBA_LANES - n_blk, tq), F32))
        ns_rows.append(jnp.zeros((HEAD_DIM - REP * MOBA_LANES, tq), F32))
        ns = jnp.concatenate(ns_rows, axis=0).T
        qa, qo = [], []
        for r in range(REP):
            mine = (lane >= HEAD_DIM + r * MOBA_LANES) & (lane < HEAD_DIM + (r + 1) * MOBA_LANES)
            qh = head(g * REP + r).astype(BF16)
            qo.append(qh)
            qa.append(qh + jnp.where(mine, ns, 0.0).astype(BF16))
        qas.append(jnp.concatenate(qa, axis=0))
        qos.append(jnp.concatenate(qo, axis=0))

    def scores(k0, width, qs):
        return [_dot_nt(qs[g], k_ref[pl.ds(k0, width), g * LANES:(g + 1) * LANES]) for g in range(N_KV)]

    pair = 2 * MOBA_BLOCK

    def body(kt, carry):
        k0 = pl.multiple_of(kt * pair, pair)
        ss = scores(k0, pair, qas)
        return tuple(_flash_step(carry[g], ss[g], v_ref[pl.ds(k0, pair), g * LANES:(g + 1) * LANES], g)
                     for g in range(N_KV))

    carry = lax.fori_loop(0, i // 2, body, tuple(_flash_init(REP * tq, g) for g in range(N_KV)))
    odd = (i % 2) == 1
    kp = pl.multiple_of(jnp.maximum(i - 1, 0) * MOBA_BLOCK, MOBA_BLOCK)
    ko = pl.multiple_of(i * MOBA_BLOCK, MOBA_BLOCK)
    s_prev = scores(kp, MOBA_BLOCK, qas)
    s_own = scores(ko, MOBA_BLOCK, qos)
    outs = []
    for g in range(N_KV):
        v = jnp.concatenate([v_ref[pl.ds(kp, MOBA_BLOCK), g * LANES:(g + 1) * LANES],
                             v_ref[pl.ds(ko, MOBA_BLOCK), g * LANES:(g + 1) * LANES]], axis=0)
        s = jnp.concatenate([jnp.where(odd, s_prev[g], NEG), jnp.where(cc <= rr, s_own[g], NEG)], axis=1)
        o = _flash_out(_flash_step(carry[g], s, v, g), g)
        outs += [o[r * tq:(r + 1) * tq] for r in range(REP)]
    o_ref[...] = _compact_heads(outs).astype(BF16)


def _moba_prompt(qm, rows, mobak, mobav, nb, seq):
    tq = MOBA_BLOCK
    nq = seq // tq
    n = qm.shape[0]
    full = lambda wd: pl.BlockSpec((seq, wd), lambda b_, i: (b_, 0))
    return pl.pallas_call(
        functools.partial(_moba_kernel, tq=tq, n_blk=nq),
        grid=(nb, nq),
        in_specs=[pl.BlockSpec((tq, QPAD_W), lambda b_, i: (b_ * nq + i, 0)),
                  full(KV_W), full(2 * LANES), full(2 * LANES)],
        out_specs=pl.BlockSpec((tq, N_HEADS * HEAD_DIM), lambda b_, i: (b_ * nq + i, 0)),
        out_shape=jax.ShapeDtypeStruct((n, N_HEADS * HEAD_DIM), BF16),
        scratch_shapes=[pltpu.VMEM((nq, LANES), F32)],
        compiler_params=_params(2),
        name="moba_prompt",
    )(qm, rows, mobak, mobav)


FFN_CHUNK = 256
FFN_TM = 1024


def _ffn_kernel(x_ref, conv_ref, nsa_ref, moba_ref, wo_ref, g_ref, wgu_ref, wd_ref, o_ref):
    mix = jnp.concatenate([conv_ref[...], nsa_ref[...], moba_ref[...]], axis=1)
    x1 = x_ref[...] + _dot(mix, wo_ref[...])
    h2 = _rms(x1, g_ref[...]).astype(BF16)
    acc = x1
    for c0 in range(0, FFN_HIDDEN, FFN_CHUNK):
        gt = _dot(h2, wgu_ref[:, c0:c0 + FFN_CHUNK])
        up = _dot(h2, wgu_ref[:, FFN_HIDDEN + c0:FFN_HIDDEN + c0 + FFN_CHUNK])
        acc = acc + _dot((gt * jax.nn.sigmoid(gt) * up).astype(BF16), wd_ref[c0:c0 + FFN_CHUNK, :])
    o_ref[...] = acc


def _ffn(x, conv_o, nsa_o, moba_o, wo, g, wgu, wd, tm):
    n = x.shape[0]
    row = lambda wd_: pl.BlockSpec((tm, wd_), lambda i: (i, 0))
    return pl.pallas_call(
        _ffn_kernel,
        grid=(n // tm,),
        in_specs=[row(D_MODEL), row(CONV_CH), row(N_HEADS * HEAD_DIM), row(N_HEADS * HEAD_DIM),
                  _const_spec((D_MODEL, D_MODEL)), _const_spec((1, D_MODEL)),
                  _const_spec((D_MODEL, 2 * FFN_HIDDEN)), _const_spec((FFN_HIDDEN, D_MODEL))],
        out_specs=row(D_MODEL),
        out_shape=jax.ShapeDtypeStruct((n, D_MODEL), F32),
        compiler_params=_params(1),
        name="mix_ffn",
    )(x, conv_o, nsa_o, moba_o, wo, g, wgu, wd)


def _norm_kernel(x_ref, g_ref, o_ref):
    o_ref[...] = _rms(x_ref[...], g_ref[...])


def _final_norm(x, g, tm):
    n = x.shape[0]
    return pl.pallas_call(
        _norm_kernel,
        grid=(n // tm,),
        in_specs=[pl.BlockSpec((tm, D_MODEL), lambda i: (i, 0)), _const_spec((1, D_MODEL))],
        out_specs=pl.BlockSpec((tm, D_MODEL), lambda i: (i, 0)),
        out_shape=jax.ShapeDtypeStruct((n, D_MODEL), F32),
        compiler_params=_params(1),
        name="final_norm",
    )(x, g)


def _q8(q, g):
    row = _iota((8, LANES), 0)
    out = jnp.zeros((8, LANES), F32)
    for r in range(REP):
        hh = g * REP + r
        out = jnp.where(row == r, q[:, hh * LANES:(hh + 1) * LANES], out)
    return pltpu.roll(out, HEAD_DIM, 1) if g == 1 else out


def _gather_pages(page_src, page_dst, sem, n_pages, start):
    def body(p, _):
        cp = pltpu.make_async_copy(page_src(p), page_dst(p), sem)
        if start:
            cp.start()
        else:
            cp.wait()
        return 0
    lax.fori_loop(0, n_pages, body, 0)


def _paged_prologue(src_of, dst_of, sem, n_pages):
    b = pl.program_id(0)
    slot = b % 2

    def run(bb, sl, start):
        _gather_pages(lambda p: src_of(bb, p), lambda p: dst_of(sl, p), sem.at[sl], n_pages, start)

    @pl.when(b == 0)
    def _():
        run(0, 0, True)

    run(b, slot, False)

    @pl.when(b + 1 < pl.num_programs(0))
    def _():
        run(b + 1, 1 - slot, True)

    return slot


def _rank_lanes(score):
    n = score.shape[1]
    s_row = jnp.broadcast_to(score, (n, n))
    s_col = s_row.T
    ii = _iota((n, n), 0)
    jj = _iota((n, n), 1)
    beats = (s_col > s_row) | ((s_col == s_row) & (ii < jj))
    return jnp.sum(jnp.where(beats, 1, 0), axis=0, keepdims=True)


def _cmp_sample_kernel(pt_ref, cache_ref, new_ref, q_ref, wbig_ref, w2_ref, pe_ref, mw_ref, perm_ref,
                       o_ref, idx_ref, x_ref, xb_ref, ab_ref, sem, *, layer, n_pages, past):
    slot = _paged_prologue(lambda b, p: cache_ref.at[layer, pt_ref[b, p]], lambda sl, p: x_ref.at[sl, p], sem, n_pages)
    per_page = PAGE // CMP_STRIDE

    def pair_to_chunks(pp):
        r0 = pl.multiple_of(pp * 2 * per_page, 2 * per_page)
        tr = []
        for t in range(2):
            xp = _dot(x_ref[slot, 2 * pp + t].astype(BF16), perm_ref[...])
            tr.append([xp[hv * LANES:(hv + 1) * LANES, :].T for hv in range(2)])
        for i in range(CMP_STRIDE):
            for hv in range(2):
                piece = jnp.concatenate([tr[t][hv][i * per_page:(i + 1) * per_page, :] for t in range(2)], axis=0)
                xb_ref[hv, pl.ds(r0, 2 * per_page), i * LANES:(i + 1) * LANES] = piece.astype(BF16)

    def to_chunks(t, _):
        for u in range(CMP_PAIRS_PER_TRIP):
            pair_to_chunks(t * CMP_PAIRS_PER_TRIP + u)
        return 0

    lax.fori_loop(0, n_pages // (2 * CMP_PAIRS_PER_TRIP), to_chunks, 0)
    n_tok = n_pages * per_page
    for hv in range(2):
        new_chunk = jnp.concatenate([new_ref[0][:, hv * LANES:(hv + 1) * LANES],
                                     jnp.zeros((1, HALF_CHUNK_W - LANES), F32)], axis=1)
        xb_ref[hv, pl.ds(n_tok, CMP_TAIL), :] = jnp.where(_iota((CMP_TAIL, 1), 0) == 0, new_chunk, 0.0).astype(BF16)
    comp = _compress(xb_ref, n_tok, wbig_ref, w2_ref, pe_ref, ab_ref)
    kc = comp[:, :LANES]
    vc = comp[:, LANES:].astype(BF16)
    q = q_ref[0]
    cmask = (_iota((1, n_tok), 1) * CMP_STRIDE + (2 * CMP_STRIDE - 1)) <= past
    row8 = _iota((8, 1), 0)
    n_sb = past // SEL_BLOCK + 1
    cur = past // SEL_BLOCK
    n_l = mw_ref.shape[1]
    jl = _iota((1, n_l), 1)
    for g in range(N_KV):
        p = _softmax_rows(_dot_nt(_q8(q, g), kc, HI), cmask)
        o_ref[0, g] = _dot(p.astype(BF16), vc)
        imp = jnp.sum(jnp.where(row8 < REP, p, 0.0), axis=0, keepdims=True)
        pslc = _dot(jnp.broadcast_to(imp, (8, n_tok)), mw_ref[...], HI)[0:1, :]
        forced = (jl == 0) | (jl == cur) | (jl == cur - 1)
        score = jnp.where(jl >= n_sb, -jnp.inf, jnp.where(forced, BIG, jnp.where(jl <= cur, pslc, NEG)))
        rank = _rank_lanes(score)
        kk = _iota((N_SEL, n_l), 0)
        idx = jnp.sum(jnp.where(rank == kk, _iota((N_SEL, n_l), 1), 0), axis=1, keepdims=True)
        idx_ref[0, g] = jnp.broadcast_to(idx, (N_SEL, LANES))


def _cmp_sample(layer, pt, cache_t, new, q, wbig, w2, pe, mw, past):
    bd, n_pages = pt.shape
    n_tok = n_pages * (PAGE // CMP_STRIDE)
    blk = lambda *s: pl.BlockSpec((1,) + s, lambda b_, pt_: (b_,) + (0,) * len(s))
    cst = lambda shape: pl.BlockSpec(shape, lambda b_, pt_: (0,) * len(shape), pipeline_mode=pl.Buffered(1))
    gs = pltpu.PrefetchScalarGridSpec(
        num_scalar_prefetch=1, grid=(bd,),
        in_specs=[pl.BlockSpec(memory_space=pl.ANY), blk(1, KV_W), blk(1, QPAD_W),
                  cst((2, HALF_CHUNK_W, CMP_AB_W)), cst((4 * CMP_HIDDEN, KV_W)), cst((2, 8, HALF_CHUNK_W)),
                  cst(mw.shape), cst((PAGE, PAGE))],
        out_specs=[blk(N_KV, 8, LANES), blk(N_KV, N_SEL, LANES)],
        scratch_shapes=[pltpu.VMEM((2, n_pages, KV_W, PAGE), F32),
                        pltpu.VMEM((2, n_tok + CMP_TAIL, HALF_CHUNK_W), BF16),
                        pltpu.VMEM((2, n_tok + CMP_TAIL, CMP_AB_W), F32), pltpu.SemaphoreType.DMA((2,))])
    return pl.pallas_call(
        functools.partial(_cmp_sample_kernel, layer=layer, n_pages=n_pages, past=past),
        grid_spec=gs,
        out_shape=[jax.ShapeDtypeStruct((bd, N_KV, 8, LANES), F32), jax.ShapeDtypeStruct((bd, N_KV, N_SEL, LANES), I32)],
        compiler_params=_params(1, 60 * 1024 * 1024),
        name="cmp_sample",
    )(pt, cache_t, new, q, wbig, w2, pe, mw, jnp.asarray(_chunk_perm(), BF16))


def _chunk_perm():
    per_page = PAGE // CMP_STRIDE
    src = np.arange(PAGE)
    dst = (src % CMP_STRIDE) * per_page + src // CMP_STRIDE
    m = np.zeros((PAGE, PAGE), np.float32)
    m[src, dst] = 1.0
    return m


def _one_query_attention(q8, tiles, k_new, v_new):
    qb = q8.astype(BF16)
    ss = []
    for k_t, _, mask in tiles:
        s = _dot(qb, k_t)
        ss.append(s if mask is None else jnp.where(mask, s, NEG))
    s_new = jnp.sum(q8 * k_new, axis=-1, keepdims=True)
    m = functools.reduce(jnp.maximum, [jnp.max(s, axis=-1, keepdims=True) for s in ss] + [s_new])
    es = [jnp.exp(s - m) for s in ss]
    e_new = jnp.exp(s_new - m)
    l = functools.reduce(jnp.add, [jnp.sum(e, axis=-1, keepdims=True) for e in es]) + e_new
    acc = functools.reduce(jnp.add, [_dot_nt(e.astype(BF16), t[1]) for e, t in zip(es, tiles)])
    return (acc + e_new * v_new) / l


def _selwin_sample_kernel(pt_ref, idx_ref, cache_ref, newsel_ref, win_ref, newwin_ref, q_ref,
                          osel_ref, owin_ref, buf_ref, sem, *, layer, n_cb):
    b = pl.program_id(0)
    per_page = PAGE // SEL_BLOCK
    slot = b % 2

    def gather(bb, sl, start):
        for g in range(N_KV):
            for k in range(N_SEL):
                blk = jnp.minimum(idx_ref[bb, g * N_SEL + k], n_cb - 1)
                cp = pltpu.make_async_copy(cache_ref.at[layer, pt_ref[bb, blk // per_page]], buf_ref.at[sl, g, k],
                                           sem.at[sl])
                if start:
                    cp.start()
                else:
                    cp.wait()

    @pl.when(b == 0)
    def _():
        gather(0, 0, True)

    @pl.when(b + 1 < pl.num_programs(0))
    def _():
        gather(b + 1, 1 - slot, True)

    q = q_ref[0]
    new_sel = newsel_ref[0]
    new_win = newwin_ref[0]
    win_k = win_ref[0, 0, :LANES, :].astype(BF16)
    win_v = win_ref[0, 0, LANES:, :].astype(BF16)
    for g in range(N_KV):
        owin_ref[0, g] = _one_query_attention(_q8(q, g), [(win_k, win_v, None)], new_win[:, :LANES], new_win[:, LANES:])
    gather(b, slot, False)
    lane_blk = _iota((1, PAGE), 1) // SEL_BLOCK
    for g in range(N_KV):
        tiles = []
        for k in range(N_SEL):
            blk = idx_ref[b, g * N_SEL + k]
            mask = (lane_blk == blk % per_page) & (blk < n_cb)
            tiles.append((buf_ref[slot, g, k, :LANES, :].astype(BF16), buf_ref[slot, g, k, LANES:, :].astype(BF16),
                          mask))
        osel_ref[0, g] = _one_query_attention(_q8(q, g), tiles, new_sel[:, :LANES], new_sel[:, LANES:])


def _selwin_sample(layer, pt, idx, cache_t, newsel, win_t, newwin, q, past):
    bd = pt.shape[0]
    wb = win_t.shape[-1]
    blk = lambda *s: pl.BlockSpec((1,) + s, lambda b_, *_: (b_,) + (0,) * len(s))
    gs = pltpu.PrefetchScalarGridSpec(
        num_scalar_prefetch=2, grid=(bd,),
        in_specs=[pl.BlockSpec(memory_space=pl.ANY), blk(1, KV_W),
                  pl.BlockSpec((1, 1, KV_W, wb), lambda b_, *_: (layer, b_, 0, 0)), blk(1, KV_W), blk(1, QPAD_W)],
        out_specs=[blk(N_KV, 8, LANES), blk(N_KV, 8, LANES)],
        scratch_shapes=[pltpu.VMEM((2, N_KV, N_SEL, KV_W, PAGE), F32), pltpu.SemaphoreType.DMA((2,))])
    return pl.pallas_call(
        functools.partial(_selwin_sample_kernel, layer=layer, n_cb=past // SEL_BLOCK),
        grid_spec=gs,
        out_shape=[jax.ShapeDtypeStruct((bd, N_KV, 8, LANES), F32)] * 2,
        compiler_params=_params(1),
        name="selwin_sample",
    )(pt, idx, cache_t, newsel, win_t, newwin, q)


MOBA_CHUNK_PAGES = 8


def _moba_sample_kernel(pt_ref, cache_ref, new_ref, q_ref, o_ref, x_ref, s_ref, sem, *, layer, n_pages, n_blk):
    slot = _paged_prologue(lambda b, p: cache_ref.at[layer, pt_ref[b, p]], lambda sl, p: x_ref.at[sl, p], sem, n_pages)
    q = q_ref[0]
    new = new_ref[0]
    per_blk = MOBA_BLOCK // PAGE
    lane_sq = _iota((LANES, LANES), 1)
    km_t = jnp.zeros((LANES, LANES), F32)
    for j in range(n_blk):
        blk = functools.reduce(jnp.add, [x_ref[slot, j * per_blk + t, :LANES, :] for t in range(per_blk)])
        km_t = jnp.where(lane_sq == j, jnp.sum(blk, axis=1, keepdims=True) * (1.0 / MOBA_BLOCK), km_t)
    jl = _iota((1, LANES), 1)
    row8 = _iota((8, LANES), 0)
    q8s, notsels = [], []
    for g in range(N_KV):
        q8 = _q8(q, g)
        gate = _dot(q8, km_t, HI)
        notsel = jnp.ones((8, LANES), F32)
        for r in range(REP):
            score = jnp.where(jl < n_blk, gate[r:r + 1, :], -jnp.inf)
            sel = (_rank_lanes(score) < MOBA_TOPK) & (jl < n_blk)
            notsel = jnp.where((row8 == r) & sel, 0.0, notsel)
        q8s.append(q8)
        notsels.append(notsel)
    cw = MOBA_CHUNK_PAGES * PAGE
    blk_per_chunk = cw // MOBA_BLOCK
    blk_of_lane = _iota((1, cw), 1) // MOBA_BLOCK
    n_chunks = n_pages // MOBA_CHUNK_PAGES
    for c in range(n_chunks):
        k_t = jnp.concatenate([x_ref[slot, c * MOBA_CHUNK_PAGES + t, :LANES, :] for t in range(MOBA_CHUNK_PAGES)],
                              axis=1).astype(BF16)
        for g in range(N_KV):
            off = jnp.zeros((8, cw), F32)
            for t in range(blk_per_chunk):
                j = c * blk_per_chunk + t
                off = jnp.where(blk_of_lane == t, notsels[g][:, j:j + 1], off)
            s_ref[g, :, c * cw:(c + 1) * cw] = jnp.where(off > 0.5, NEG, _dot(q8s[g].astype(BF16), k_t))
    ls, e_news = [], []
    for g in range(N_KV):
        s = s_ref[g]
        s_new = jnp.sum(q8s[g] * new[:, :LANES], axis=-1, keepdims=True)
        m = jnp.maximum(jnp.max(s, axis=-1, keepdims=True), s_new)
        e = jnp.exp(s - m)
        e_new = jnp.exp(s_new - m)
        ls.append(jnp.sum(e, axis=-1, keepdims=True) + e_new)
        e_news.append(e_new)
        s_ref[g] = e
    acc = [e_news[g] * new[:, LANES:] for g in range(N_KV)]
    for c in range(n_chunks):
        v_t = jnp.concatenate([x_ref[slot, c * MOBA_CHUNK_PAGES + t, LANES:, :] for t in range(MOBA_CHUNK_PAGES)],
                              axis=1).astype(BF16)
        for g in range(N_KV):
            acc[g] = acc[g] + _dot_nt(s_ref[g, :, c * cw:(c + 1) * cw].astype(BF16), v_t)
    for g in range(N_KV):
        o_ref[0, g] = acc[g] / ls[g]


def _moba_sample(layer, pt, cache_t, new, q, past):
    bd, n_pages = pt.shape
    n_blk = past // MOBA_BLOCK
    blk = lambda *s: pl.BlockSpec((1,) + s, lambda b_, pt_: (b_,) + (0,) * len(s))
    gs = pltpu.PrefetchScalarGridSpec(
        num_scalar_prefetch=1, grid=(bd,),
        in_specs=[pl.BlockSpec(memory_space=pl.ANY), blk(1, KV_W), blk(1, QPAD_W)],
        out_specs=blk(N_KV, 8, LANES),
        scratch_shapes=[pltpu.VMEM((2, n_pages, KV_W, PAGE), F32), pltpu.VMEM((N_KV, 8, past), F32),
                        pltpu.SemaphoreType.DMA((2,))])
    return pl.pallas_call(
        functools.partial(_moba_sample_kernel, layer=layer, n_pages=n_pages, n_blk=n_blk),
        grid_spec=gs,
        out_shape=jax.ShapeDtypeStruct((bd, N_KV, 8, LANES), F32),
        compiler_params=_params(1),
        name="moba_sample",
    )(pt, cache_t, new, q)


def _combine_sample_kernel(u_ref, st_ref, w_ref, b_ref, lg_ref, lb_ref, gates_ref, ocmp_ref, osel_ref, owin_ref,
                           omoba_ref, conv_ref, nsa_ref, moba_ref):
    y = u_ref[...] * w_ref[CONV_WIDTH - 1:CONV_WIDTH, :]
    for j in range(CONV_WIDTH - 1):
        y = y + st_ref[j] * w_ref[j:j + 1, :]
    conv_ref[...] = _conv_post(y, b_ref, lg_ref, lb_ref).astype(BF16)
    gates = gates_ref[...]
    w = N_HEADS * HEAD_DIM
    head_of_lane = _iota((1, w), 1) // HEAD_DIM
    acc = jnp.zeros(ocmp_ref.shape, F32)
    for br, ref in enumerate((ocmp_ref, osel_ref, owin_ref)):
        ge = jnp.zeros(ocmp_ref.shape, F32)
        for hh in range(N_HEADS):
            c = br * N_HEADS + hh
            ge = jnp.where(head_of_lane == hh, gates[:, c:c + 1], ge)
        acc = acc + ge * ref[...]
    nsa_ref[...] = acc.astype(BF16)
    moba_ref[...] = omoba_ref[...].astype(BF16)


def _combine_sample(u, st_t, w, b, lg, lb, gates, o_cmp, o_sel, o_win, o_moba):
    bd = u.shape[0]
    w_h = N_HEADS * HEAD_DIM
    return pl.pallas_call(
        _combine_sample_kernel,
        out_shape=[jax.ShapeDtypeStruct((bd, CONV_CH), BF16), jax.ShapeDtypeStruct((bd, w_h), BF16),
                   jax.ShapeDtypeStruct((bd, w_h), BF16)],
        name="combine_sample",
    )(u, st_t, w, b, lg, lb, gates, o_cmp, o_sel, o_win, o_moba)


def _rope_tables(pos):
    half = HEAD_DIM // 2
    inv = ROPE_THETA ** (-jnp.arange(half, dtype=F32) / half)
    ang = pos.astype(F32)[:, None] * inv[None, :]
    cos, sin = jnp.cos(ang), jnp.sin(ang)
    zero = jnp.zeros_like(sin)
    return (jnp.tile(cos, (1, 4)), jnp.tile(jnp.concatenate([-sin, zero], axis=1), (1, 2)),
            jnp.tile(jnp.concatenate([zero, sin], axis=1), (1, 2)))


def _pad_heads(w):
    d = w.shape[:-1]
    w = w.reshape(*d, N_HEADS, HEAD_DIM)
    return jnp.pad(w, [(0, 0)] * len(d) + [(0, 0), (0, LANES - HEAD_DIM)]).reshape(*d, QPAD_W)


def _relayout_w_in(w_in):
    c_q = 2 * CONV_CH
    c_kv = c_q + N_HEADS * HEAD_DIM
    c_g = c_kv + 6 * N_KV * HEAD_DIM
    c_qm = c_g + 3 * N_HEADS
    c_kvm = c_qm + N_HEADS * HEAD_DIM
    gates = jnp.pad(w_in[..., c_g:c_qm], ((0, 0), (0, 0), (0, LANES - 3 * N_HEADS)))
    return jnp.concatenate([w_in[..., :c_q], _pad_heads(w_in[..., c_q:c_kv]), w_in[..., c_kv:c_g],
                            _pad_heads(w_in[..., c_qm:c_kvm]), w_in[..., c_kvm:], gates], axis=-1).astype(BF16)


def _compress_weights(pe_k, pe_v, wk1, wk2, wv1, wv2):
    depth = wk1.shape[0]
    w1 = jnp.stack([wk1, wv1], axis=1).reshape(depth, 2, 2, CMP_STRIDE, HEAD_DIM, CMP_HIDDEN)
    wkv = jnp.einsum('lkaidh,gt->lkigdath', w1, jnp.eye(N_KV, dtype=F32))
    wkv = wkv.reshape(depth, 2, HALF_CHUNK_W, CMP_AB_W).astype(BF16)
    w2 = jnp.stack([wk2, wk2, wv2, wv2], axis=1)
    w2big = jnp.einsum('lshd,st->lshtd', w2, jnp.eye(4, dtype=F32)).reshape(depth, 4 * CMP_HIDDEN, KV_W).astype(BF16)
    pe = jnp.stack([pe_k, pe_v], axis=1).reshape(depth, 2, 2, CMP_STRIDE, 1, HEAD_DIM)
    pe = jnp.broadcast_to(pe, (depth, 2, 2, CMP_STRIDE, N_KV, HEAD_DIM)).reshape(depth, 2, 2, HALF_CHUNK_W)
    return wkv, w2big, jnp.pad(pe, ((0, 0), (0, 0), (0, 6), (0, 0)))


def _slc_matrix(n_c, n_sb):
    ratio = SEL_BLOCK // CMP_STRIDE
    c = np.arange(n_c)[:, None]
    j = np.arange(n_sb)[None, :]
    m = np.where(c == ratio * j, 1.0, 0.0) + np.where((c > ratio * j) & (c < ratio * (j + 1)), 2.0, 0.0) \
        + np.where(c == ratio * (j + 1), 1.0, 0.0)
    return m.astype(np.float32)


def _heads_from_q8(o):
    return jnp.concatenate([o[:, 0, :REP, :HEAD_DIM], o[:, 1, :REP, HEAD_DIM:]], axis=1).reshape(o.shape[0], -1)


def kernel(x_prompt, x_sample, cache_nsa_cmp, cache_nsa_sel, cache_moba, state_nsa_win, state_conv, page_table,
           g_mix, w_in, conv_w, conv_b, conv_ln_g, conv_ln_b, cmp_pe_k, cmp_pe_v, cmp_wk1, cmp_wk2, cmp_wv1, cmp_wv2,
           w_out, g_ffn, w_gate_up, w_down, g_final):
    nb, seq, _ = x_prompt.shape
    bd, dec_seq, _ = x_sample.shape
    depth = g_mix.shape[0]
    n_pages = page_table.shape[1]
    past = n_pages * PAGE
    assert dec_seq == 1 and seq % 512 == 0 and seq // SEL_BLOCK <= HEAD_DIM and 3 * (seq // MOBA_BLOCK) <= HEAD_DIM
    assert past % MOBA_BLOCK == 0 and past // MOBA_BLOCK <= LANES and bd % 8 == 0 and n_pages % MOBA_CHUNK_PAGES == 0
    assert n_pages % (2 * CMP_PAIRS_PER_TRIP) == 0
    n_p = nb * seq
    tm = 512

    w_in_r = _relayout_w_in(w_in)
    wbig, w2big, pe_rows = _compress_weights(cmp_pe_k, cmp_pe_v, cmp_wk1, cmp_wk2, cmp_wv1, cmp_wv2)
    w_out_b, w_gu_b, w_down_b = w_out.astype(BF16), w_gate_up.astype(BF16), w_down.astype(BF16)
    conv_w_p = jnp.pad(conv_w, ((0, 0), (0, CONV_HALO - CONV_WIDTH), (0, 0)))
    tab_p = _rope_tables(jnp.arange(seq))
    tab_s = _rope_tables(jnp.full((bd,), past))
    n_c = seq // CMP_STRIDE
    mwt_p = jnp.asarray(_slc_matrix(n_c, seq // SEL_BLOCK).T)
    n_sb_s = past // SEL_BLOCK + 1
    mw_s = jnp.asarray(np.pad(_slc_matrix(past // CMP_STRIDE, n_sb_s), ((0, 0), (0, -n_sb_s % LANES))))

    feat_major = lambda a: a.transpose(0, 1, 3, 4, 5, 2).reshape(a.shape[0], a.shape[1], KV_W, a.shape[2])
    cmp_t, sel_t, moba_t, win_t = (feat_major(a) for a in (cache_nsa_cmp, cache_nsa_sel, cache_moba, state_nsa_win))

    xp = x_prompt.reshape(n_p, D_MODEL)
    xs = x_sample.reshape(bd, D_MODEL)
    new_p = [[], [], [], [], []]
    new_s = [[], [], [], [], []]
    r2 = lambda a: a.reshape(1, -1)
    for l in range(depth):
        (u, q, qm, gates, cmp_rows, moba_rows, selk, selv, wink, winv, mobak, mobav,
         cmp_n, sel_n, win_n, moba_n) = _proj(xp, r2(g_mix[l]), w_in_r[l], *tab_p, seq, tm, True)
        conv_o = _conv(u, conv_w_p[l], r2(conv_b[l]), r2(conv_ln_g[l]), r2(conv_ln_b[l]), nb, seq, 512)
        comp = _compress_prompt(cmp_rows, wbig[l], w2big[l], pe_rows[l], nb, seq)
        nsa_o = _nsa_prompt(q, gates, comp, selk, selv, wink, winv, mwt_p, nb, seq)
        moba_o = _moba_prompt(qm, moba_rows, mobak, mobav, nb, seq)
        xp = _ffn(xp, conv_o, nsa_o, moba_o, w_out_b[l], r2(g_ffn[l]), w_gu_b[l], w_down_b[l], FFN_TM)
        kv6 = lambda a, n_: a.reshape(n_, -1, 2, N_KV, HEAD_DIM)
        from_feat = lambda a: a.reshape(nb, 2, N_KV, HEAD_DIM, -1).transpose(0, 4, 1, 2, 3)
        new_p[0].append(from_feat(cmp_n))
        new_p[1].append(from_feat(sel_n))
        new_p[2].append(from_feat(moba_n))
        new_p[3].append(from_feat(win_n[:, :, -min(WINDOW, seq):]))
        new_p[4].append(u.reshape(nb, seq, CONV_CH)[:, -(CONV_WIDTH - 1):])

        (u, q, qm, gates, cmp_rows, sel_rows, win_rows, moba_rows) = _proj(xs, r2(g_mix[l]), w_in_r[l], *tab_s, bd, bd,
                                                                           False)
        b3 = lambda a: a.reshape(bd, 1, -1)
        o_cmp, idx = _cmp_sample(l, page_table, cmp_t, b3(cmp_rows), b3(q), wbig[l], w2big[l], pe_rows[l],
                                 mw_s, past)
        o_sel, o_win = _selwin_sample(l, page_table, idx[:, :, :, 0].reshape(bd, N_KV * N_SEL), sel_t,
                                      b3(sel_rows), win_t, b3(win_rows), b3(q), past)
        o_moba = _moba_sample(l, page_table, moba_t, b3(moba_rows), b3(qm), past)
        conv_o, nsa_o, moba_o = _combine_sample(
            u, state_conv[l].transpose(1, 0, 2), conv_w_p[l], r2(conv_b[l]), r2(conv_ln_g[l]), r2(conv_ln_b[l]), gates,
            _heads_from_q8(o_cmp), _heads_from_q8(o_sel), _heads_from_q8(o_win), _heads_from_q8(o_moba))
        xs = _ffn(xs, conv_o, nsa_o, moba_o, w_out_b[l], r2(g_ffn[l]), w_gu_b[l], w_down_b[l], bd)
        new_s[0].append(kv6(cmp_rows, bd))
        new_s[1].append(kv6(sel_rows, bd))
        new_s[2].append(kv6(moba_rows, bd))
        new_s[3].append(jnp.concatenate([state_nsa_win[l], kv6(win_rows, bd)], axis=1)[:, -state_nsa_win.shape[2]:])
        new_s[4].append(jnp.concatenate([state_conv[l], u[:, None, :]], axis=1)[:, -(CONV_WIDTH - 1):])

    y_prompt = _final_norm(xp, r2(g_final), tm).reshape(nb, seq, D_MODEL)
    y_sample = _final_norm(xs, r2(g_final), bd).reshape(bd, 1, D_MODEL)
    return (y_prompt, y_sample,
            jnp.stack(new_p[0]), jnp.stack(new_s[0]),
            jnp.stack(new_p[1]), jnp.stack(new_s[1]),
            jnp.stack(new_p[2]), jnp.stack(new_s[2]),
            jnp.stack(new_p[3]), jnp.stack(new_s[3]),
            jnp.stack(new_p[4]), jnp.stack(new_s[4]))
```

```python
import functools

import numpy as np
import jax
import jax.numpy as jnp
from jax import lax
from jax.experimental import pallas as pl
from jax.experimental.pallas import tpu as pltpu

F32 = jnp.float32
BF16 = jnp.bfloat16
I32 = jnp.int32

D_MODEL = 1024
HEAD_DIM = 64
LANES = 128
CONV_CH = 256
N_HEADS = 6
N_KV = 2
REP = N_HEADS // N_KV
CONV_WIDTH = 31
CMP_STRIDE = 16
CMP_HIDDEN = 128
SEL_BLOCK = 64
N_SEL = 16
WINDOW = 512
BAND = 128
MOBA_BLOCK = 256
MOBA_TOPK = 3
PAGE = 128
FFN_HIDDEN = 2816
ROPE_THETA = 10000.0
EPS = 1e-6
NEG = -1e30
BIG = 1e30
ATTN_SCALE = HEAD_DIM ** -0.5
HI = lax.Precision.HIGHEST

KV_W = 2 * N_KV * HEAD_DIM
QPAD_W = N_HEADS * LANES
CHUNK_W = CMP_STRIDE * KV_W

OFF_U, OFF_Q, OFF_KV, OFF_QM, OFF_KVM, OFF_G, W_IN_COLS = 0, 512, 1280, 2048, 2816, 3072, 3200

VMEM_LIMIT = 56 * 1024 * 1024


def _params(n_axes, limit=VMEM_LIMIT):
    return pltpu.CompilerParams(dimension_semantics=("arbitrary",) * n_axes, vmem_limit_bytes=limit)


def _const_spec(shape):
    nd = len(shape)
    return pl.BlockSpec(shape, lambda *_: (0,) * nd, pipeline_mode=pl.Buffered(1))


def _dot(a, b, precision=None):
    return jnp.dot(a, b, preferred_element_type=F32, precision=precision)


def _dot_nt(a, b, precision=None):
    return lax.dot_general(a, b, (((1,), (1,)), ((), ())), preferred_element_type=F32, precision=precision)


def _split_bf16(x):
    hi = x.astype(BF16)
    return hi, (x - hi.astype(F32)).astype(BF16)


def _dot_nt_3pass(a, b):
    a_hi, a_lo = _split_bf16(a)
    b_hi, b_lo = _split_bf16(b)
    return _dot_nt(a_hi, b_hi) + (_dot_nt(a_hi, b_lo) + _dot_nt(a_lo, b_hi))


def _iota(shape, dim):
    return lax.broadcasted_iota(I32, shape, dim)


def _rms(x, g):
    return x * lax.rsqrt(jnp.mean(x * x, axis=-1, keepdims=True) + EPS) * g


def _compact_heads(a):
    lo = _iota(a[0].shape, 1) < HEAD_DIM
    return jnp.concatenate([
        jnp.where(lo, a[0], pltpu.roll(a[1], HEAD_DIM, 1)),
        jnp.where(lo, a[2], a[3]),
        jnp.where(lo, pltpu.roll(a[4], HEAD_DIM, 1), a[5])], axis=1)


def _proj_kernel(x_ref, g_ref, w_ref, cos_ref, sa_ref, sb_ref, u_ref, q_ref, qm_ref, gates_ref, *refs,
                 seq, tm, prompt):
    if prompt:
        (cmp_ref, moba_ref, selk_ref, selv_ref, wink_ref, winv_ref, mobak_ref, mobav_ref,
         cmp_t_ref, sel_t_ref, win_t_ref, moba_t_ref) = refs
        sel_ref = win_ref = None
    else:
        cmp_ref, sel_ref, win_ref, moba_ref = refs
        cmp_t_ref = sel_t_ref = win_t_ref = moba_t_ref = None
    i = pl.program_id(0)
    h = _rms(x_ref[...], g_ref[...]).astype(BF16)
    cos, sa, sb = cos_ref[...], sa_ref[...], sb_ref[...]

    def seg(off, width):
        return _dot(h, w_ref[:, off:off + width])

    def rope(z):
        return z * cos + pltpu.roll(z, LANES - 32, 1) * sa + pltpu.roll(z, 32, 1) * sb

    zu = seg(OFF_U, 2 * CONV_CH)
    u_ref[...] = zu[:, :CONV_CH] * jax.nn.sigmoid(zu[:, CONV_CH:])

    for off, ref in ((OFF_Q, q_ref), (OFF_QM, qm_ref)):
        z = seg(off, QPAD_W)
        for hh in range(N_HEADS):
            ref[:, hh * LANES:(hh + 1) * LANES] = rope(z[:, hh * LANES:(hh + 1) * LANES]) * ATTN_SCALE

    t = (i * tm) % seq + _iota((tm, LANES), 0)
    lane = _iota((tm, LANES), 1)
    lo = lane < HEAD_DIM
    oh_sel = jnp.where((lane >= HEAD_DIM) & ((t >> 6) == lane - HEAD_DIM), NEG, 0.0)
    oh_moba = jnp.where((lane >= HEAD_DIM) & (lane < HEAD_DIM + REP * MOBA_LANES)
                        & ((t >> 8) == ((lane - HEAD_DIM) & (MOBA_LANES - 1))), NEG, 0.0)

    def rows(off, ref, t_ref):
        z = seg(off, KV_W)
        k = rope(z[:, :LANES])
        v = z[:, LANES:]
        if ref is not None:
            ref[:, :LANES] = k
            ref[:, LANES:] = v
        if t_ref is not None:
            t_ref[0, :LANES, :] = k.T
            t_ref[0, LANES:, :] = v.T
        return k, v

    rows(OFF_KV, cmp_ref, cmp_t_ref)
    k, v = rows(OFF_KV + KV_W, sel_ref, sel_t_ref)
    if prompt:
        selk_ref[:, :LANES] = jnp.where(lo, k, oh_sel).astype(BF16)
        selk_ref[:, LANES:] = jnp.where(lo, pltpu.roll(k, HEAD_DIM, 1), oh_sel).astype(BF16)
        selv_ref[:, :LANES] = jnp.where(lo, v, 1.0).astype(BF16)
        selv_ref[:, LANES:] = jnp.where(lo, 1.0, v).astype(BF16)
    k, v = rows(OFF_KV + 2 * KV_W, win_ref, win_t_ref)
    if prompt:
        wink_ref[:, :LANES] = k.astype(BF16)
        wink_ref[:, LANES:] = pltpu.roll(k, HEAD_DIM, 1).astype(BF16)
        winv_ref[...] = v.astype(BF16)
    k, v = rows(OFF_KVM, moba_ref, moba_t_ref)
    if prompt:
        mobak_ref[:, :LANES] = jnp.where(lo, k, oh_moba).astype(BF16)
        mobak_ref[:, LANES:] = jnp.where(lo, pltpu.roll(k, HEAD_DIM, 1), oh_moba).astype(BF16)
        mobav_ref[:, :LANES] = jnp.where(lo, v, 1.0).astype(BF16)
        mobav_ref[:, LANES:] = jnp.where(lo, 1.0, v).astype(BF16)

    gates_ref[...] = jax.nn.sigmoid(seg(OFF_G, LANES))


def _proj(x, g, w, cos, sa, sb, seq, tm, prompt):
    n = x.shape[0]
    nt = cos.shape[0] // tm
    row = lambda wd: pl.BlockSpec((tm, wd), lambda i: (i, 0))
    tab = pl.BlockSpec((tm, LANES), lambda i: (i % nt, 0))
    shapes = [((n, wd), F32) for wd in (CONV_CH, QPAD_W, QPAD_W, LANES)]
    specs = [row(s[1]) for s, _ in shapes]
    if prompt:
        extra = [((n, KV_W), F32)] * 2 + [((n, wd), BF16) for wd in (2 * LANES, 2 * LANES, 2 * LANES, LANES,
                                                                     2 * LANES, 2 * LANES)]
        shapes += extra + [((n // seq, KV_W, seq), F32)] * 4
        specs += [row(s[1]) for s, _ in extra]
        specs += [pl.BlockSpec((1, KV_W, tm), lambda i: (i // nt, 0, i % nt))] * 4
    else:
        shapes += [((n, KV_W), F32)] * 4
        specs += [row(KV_W)] * 4
    return pl.pallas_call(
        functools.partial(_proj_kernel, seq=seq, tm=tm, prompt=prompt),
        grid=(n // tm,),
        in_specs=[row(D_MODEL), _const_spec((1, D_MODEL)), _const_spec((D_MODEL, W_IN_COLS)), tab, tab, tab],
        out_specs=specs,
        out_shape=[jax.ShapeDtypeStruct(s, d) for s, d in shapes],
        compiler_params=_params(1),
        name="proj",
    )(x, g, w, cos, sa, sb)


CONV_HALO = 32


def _conv_post(y, b_ref, lg_ref, lb_ref):
    y = y + b_ref[...]
    mu = jnp.mean(y, axis=-1, keepdims=True)
    var = jnp.mean(jnp.square(y - mu), axis=-1, keepdims=True)
    y = (y - mu) * lax.rsqrt(var + EPS) * lg_ref[...] + lb_ref[...]
    return y * jax.nn.sigmoid(y)


def _conv_kernel(u_ref, prev_ref, w_ref, b_ref, lg_ref, lb_ref, o_ref, ext_ref, *, tq):
    i = pl.program_id(1)
    ext_ref[:CONV_HALO, :] = jnp.where(i > 0, prev_ref[...], 0.0)
    ext_ref[CONV_HALO:, :] = u_ref[...]
    off = CONV_HALO - (CONV_WIDTH - 1)
    y = jnp.zeros((tq, CONV_CH), F32)
    for j in range(CONV_WIDTH):
        y = y + ext_ref[pl.ds(off + j, tq), :] * w_ref[j:j + 1, :]
    o_ref[...] = _conv_post(y, b_ref, lg_ref, lb_ref).astype(BF16)


def _conv(u, w, b, lg, lb, nb, seq, tq):
    n = u.shape[0]
    nq = seq // tq
    per = tq // CONV_HALO
    return pl.pallas_call(
        functools.partial(_conv_kernel, tq=tq),
        grid=(nb, nq),
        in_specs=[pl.BlockSpec((tq, CONV_CH), lambda b_, i: (b_ * nq + i, 0)),
                  pl.BlockSpec((CONV_HALO, CONV_CH), lambda b_, i: (jnp.maximum((b_ * nq + i) * per - 1, 0), 0)),
                  _const_spec((CONV_HALO, CONV_CH)), _const_spec((1, CONV_CH)), _const_spec((1, CONV_CH)),
                  _const_spec((1, CONV_CH))],
        out_specs=pl.BlockSpec((tq, CONV_CH), lambda b_, i: (b_ * nq + i, 0)),
        out_shape=jax.ShapeDtypeStruct((n, CONV_CH), BF16),
        scratch_shapes=[pltpu.VMEM((tq + CONV_HALO, CONV_CH), F32)],
        compiler_params=_params(2),
        name="conv",
    )(u, u, w, b, lg, lb)


def _gelu_tanh(x):
    return x * (0.5 * (1.0 + jnp.tanh(0.7978845608028654 * (x + 0.044715 * (x * x * x)))))


CMP_TAIL = 16
CMP_PAIRS_PER_TRIP = 4


HALF_CHUNK_W = CHUNK_W // 2
CMP_AB_W = 4 * CMP_HIDDEN


def _compress(xb_ref, n_tok, wkv_ref, w2_ref, pe_ref, ab_ref):
    half = CMP_AB_W // 2
    hid = []
    for kv in range(2):
        ab_ref[kv] = _dot(xb_ref[kv], wkv_ref[kv])
        pw = _dot(pe_ref[kv].astype(BF16), wkv_ref[kv])
        pe_all = pw[0:1, :half] + pw[1:2, half:]
        hid.append(ab_ref[kv, pl.ds(0, n_tok), :half] + ab_ref[kv, pl.ds(1, n_tok), half:] + pe_all)
    return _dot(_gelu_tanh(jnp.concatenate(hid, axis=1)).astype(BF16), w2_ref[...])


def _compress_kernel(x_ref, wkv_ref, w2_ref, pe_ref, o_ref, x_scr, ab_ref, *, n_tok):
    x = x_ref[0]
    for kv in range(2):
        x_scr[kv, pl.ds(0, n_tok), :] = jnp.concatenate(
            [x[:, i * KV_W + kv * LANES:i * KV_W + (kv + 1) * LANES] for i in range(CMP_STRIDE)], axis=1).astype(BF16)
        x_scr[kv, pl.ds(n_tok, CMP_TAIL), :] = jnp.zeros((CMP_TAIL, HALF_CHUNK_W), BF16)
    o_ref[0] = _compress(x_scr, n_tok, wkv_ref, w2_ref, pe_ref, ab_ref)


def _compress_prompt(rows, wkv, w2, pe, nb, seq):
    n_tok = seq // CMP_STRIDE
    x = rows.reshape(nb, n_tok, CHUNK_W)
    return pl.pallas_call(
        functools.partial(_compress_kernel, n_tok=n_tok),
        grid=(nb,),
        in_specs=[pl.BlockSpec((1, n_tok, CHUNK_W), lambda b_: (b_, 0, 0)),
                  _const_spec((2, HALF_CHUNK_W, CMP_AB_W)), _const_spec((4 * CMP_HIDDEN, KV_W)),
                  _const_spec((2, 8, HALF_CHUNK_W))],
        out_specs=pl.BlockSpec((1, n_tok, KV_W), lambda b_: (b_, 0, 0)),
        out_shape=jax.ShapeDtypeStruct((nb, n_tok, KV_W), F32),
        scratch_shapes=[pltpu.VMEM((2, n_tok + CMP_TAIL, HALF_CHUNK_W), BF16),
                        pltpu.VMEM((2, n_tok + CMP_TAIL, CMP_AB_W), F32)],
        compiler_params=_params(1),
        name="compress_prompt",
    )(x, wkv, w2, pe)


def _softmax_rows(s, mask):
    s = jnp.where(mask, s, NEG)
    m = jnp.max(s, axis=-1, keepdims=True)
    e = jnp.where(mask, jnp.exp(s - m), 0.0)
    l = jnp.sum(e, axis=-1, keepdims=True)
    return e / jnp.where(l > 0.0, l, 1.0)


def _rank_rows(score, n):
    sub = 8
    tiles = [score[sub * r:sub * (r + 1), :] for r in range(n // sub)]
    ranks = [jnp.zeros(t.shape, I32) for t in tiles]
    jj = _iota(tiles[0].shape, 0)
    for ii in range(n):
        row = score[ii:ii + 1, :]
        for r, t in enumerate(tiles):
            if sub * r + sub - 1 < ii:
                beats = row > t
            elif sub * r > ii:
                beats = row >= t
            else:
                beats = (row > t) | ((row == t) & (ii < sub * r + jj))
            ranks[r] = ranks[r] + jnp.where(beats, 1, 0)
    return jnp.concatenate(ranks, axis=0)


def _flash_step(carry, s, v, g):
    m, c = carry
    m_new = jnp.maximum(m, jnp.max(s, axis=-1, keepdims=True))
    p = jnp.exp(s - m_new)
    return m_new, jnp.exp(m - m_new) * c + _dot(p.astype(BF16), v)


def _flash_init(rows, g):
    return jnp.full((rows, 1), NEG, F32), jnp.zeros((rows, LANES), F32)


def _flash_out(carry, g):
    c = carry[1]
    ll = (1 - g) * HEAD_DIM
    return c / c[:, ll:ll + 1]


NSA_TQ = 256
SEL_TK = 512


def _nsa_kernel(q_ref, gates_ref, comp_ref, selk_ref, selv_ref, wink_ref, winv_ref, mwt_ref, o_ref, *, tq):
    i = pl.program_id(1)
    t0 = i * tq
    q = q_ref[...]
    comp = comp_ref[0]
    n_c = comp.shape[0]
    kc = comp[:, :LANES]
    kc_g = (kc, pltpu.roll(kc, HEAD_DIM, 1))
    vc = comp[:, LANES:].astype(BF16)
    pos_col = t0 + _iota((tq, 1), 0)
    cmask = (_iota((1, n_c), 1) * CMP_STRIDE + (2 * CMP_STRIDE - 1)) <= pos_col
    n_sb = mwt_ref.shape[0]

    def head(hh):
        return q[:, hh * LANES:(hh + 1) * LANES]

    o_cmp, ns = [], []
    for g in range(N_KV):
        imp = jnp.zeros((tq, n_c), F32)
        for r in range(REP):
            p = _softmax_rows(_dot_nt_3pass(head(g * REP + r), kc_g[g]), cmask)
            imp = imp + p
            o_cmp.append(_dot(p.astype(BF16), vc))
        pslc_t = _dot_nt(mwt_ref[...], imp, HI)
        jj = _iota((n_sb, tq), 0)
        cur = (t0 + _iota((n_sb, tq), 1)) >> 6
        forced = (jj == 0) | (jj == cur) | (jj == cur - 1)
        score = jnp.where(forced, BIG, jnp.where(jj <= cur, pslc_t, NEG))
        notsel = jnp.where(_rank_rows(score, n_sb) < N_SEL, 0.0, 1.0)
        parts = [jnp.zeros((HEAD_DIM, tq), F32), notsel]
        if n_sb < HEAD_DIM:
            parts.append(jnp.zeros((HEAD_DIM - n_sb, tq), F32))
        ns.append(jnp.concatenate(parts, axis=0).T.astype(BF16))

    row_pos = t0 + (_iota((REP * tq, 1), 0) % tq)

    qas = [jnp.concatenate([head(g * REP + r).astype(BF16) + ns[g] for r in range(REP)], axis=0) for g in range(N_KV)]

    def tile(kt):
        k0 = pl.multiple_of(kt * SEL_TK, SEL_TK)
        v = [selv_ref[pl.ds(k0, SEL_TK), g * LANES:(g + 1) * LANES] for g in range(N_KV)]
        return [_dot_nt(qas[g], selk_ref[pl.ds(k0, SEL_TK), g * LANES:(g + 1) * LANES]) for g in range(N_KV)], v

    def body(kt, carry):
        ss, v = tile(kt)
        return tuple(_flash_step(carry[g], ss[g], v[g], g) for g in range(N_KV))

    last = t0 // SEL_TK
    carry = lax.fori_loop(0, last, body, tuple(_flash_init(REP * tq, g) for g in range(N_KV)))
    ss, v = tile(last)
    kpos = last * SEL_TK + _iota((1, SEL_TK), 1)
    o_sel = []
    for g in range(N_KV):
        o = _flash_out(_flash_step(carry[g], jnp.where(kpos <= row_pos, ss[g], NEG), v[g], g), g)
        o_sel += [o[r * tq:(r + 1) * tq] for r in range(REP)]

    o_win = []
    n_w = (WINDOW + tq) // BAND
    cc = _iota((1, BAND), 1)
    rr = _iota((REP * tq, 1), 0) % tq
    for g in range(N_KV):
        qw = jnp.concatenate([head(g * REP + r).astype(BF16) for r in range(REP)], axis=0)
        ss, vs = [], []
        for j in range(n_w):
            k0 = t0 - WINDOW + j * BAND
            ok = k0 >= 0
            k0c = pl.multiple_of(jnp.maximum(k0, 0), BAND)
            s = _dot_nt(qw, wink_ref[pl.ds(k0c, BAND), g * LANES:(g + 1) * LANES])
            if BAND * j < tq:
                ok = ok & (cc >= rr - BAND * j)
            if WINDOW - BAND * j < BAND:
                ok = ok & (cc <= rr + (WINDOW - BAND * j))
            ss.append(jnp.where(ok, s, NEG))
            vs.append(winv_ref[pl.ds(k0c, BAND), :])
        m = functools.reduce(jnp.maximum, [jnp.max(s, axis=-1, keepdims=True) for s in ss])
        es = [jnp.exp(s - m) for s in ss]
        l = functools.reduce(jnp.add, [jnp.sum(e, axis=-1, keepdims=True) for e in es])
        acc = functools.reduce(jnp.add, [_dot(e.astype(BF16), v) for e, v in zip(es, vs)])
        o = acc / l
        o_win += [o[r * tq:(r + 1) * tq] for r in range(REP)]

    gates = gates_ref[...]
    mixed = []
    for hh in range(N_HEADS):
        gc = [gates[:, br * N_HEADS + hh:br * N_HEADS + hh + 1] for br in range(3)]
        mixed.append(gc[0] * o_cmp[hh] + gc[1] * o_sel[hh] + gc[2] * o_win[hh])
    o_ref[...] = _compact_heads(mixed).astype(BF16)


def _nsa_prompt(q, gates, comp, selk, selv, wink, winv, mwt, nb, seq):
    tq = NSA_TQ
    nq = seq // tq
    n = q.shape[0]
    n_c = comp.shape[1]
    row = lambda wd: pl.BlockSpec((tq, wd), lambda b_, i: (b_ * nq + i, 0))
    full = lambda wd: pl.BlockSpec((seq, wd), lambda b_, i: (b_, 0))
    return pl.pallas_call(
        functools.partial(_nsa_kernel, tq=tq),
        grid=(nb, nq),
        in_specs=[row(QPAD_W), row(LANES), pl.BlockSpec((1, n_c, KV_W), lambda b_, i: (b_, 0, 0)),
                  full(2 * LANES), full(2 * LANES), full(2 * LANES), full(LANES), _const_spec(mwt.shape)],
        out_specs=row(N_HEADS * HEAD_DIM),
        out_shape=jax.ShapeDtypeStruct((n, N_HEADS * HEAD_DIM), BF16),
        compiler_params=_params(2),
        name="nsa_prompt",
    )(q, gates, comp, selk, selv, wink, winv, mwt)


MOBA_LANES = 16


def _moba_kernel(q_ref, rows_ref, k_ref, v_ref, o_ref, kmean_ref, *, tq, n_blk):
    i = pl.program_id(1)

    @pl.when(i == 0)
    def _():
        for j in range(n_blk):
            kmean_ref[j:j + 1, :] = jnp.sum(rows_ref[j * MOBA_BLOCK:(j + 1) * MOBA_BLOCK, :LANES],
                                            axis=0, keepdims=True) * (1.0 / MOBA_BLOCK)

    q = q_ref[...]
    km = kmean_ref[...]
    km_g = (km, pltpu.roll(km, HEAD_DIM, 1))
    jj = _iota((n_blk, tq), 0)
    lane = _iota((tq, LANES), 1)
    cc = _iota((1, MOBA_BLOCK), 1)
    rr = _iota((REP * tq, 1), 0) % tq

    def head(hh):
        return q[:, hh * LANES:(hh + 1) * LANES]

    qas, qos = [], []
    for g in range(N_KV):
        ns_rows = [jnp.zeros((HEAD_DIM, tq), F32)]
        for r in range(REP):
            gate = jnp.where(jj < i, _dot_nt(km_g[g], head(g * REP + r), HI), NEG)
            sel = (_rank_rows(gate, n_blk) < MOBA_TOPK) & (jj < i)
            ns_rows.append(jnp.where(sel, 0.0, 1.0))
            if n_blk < MOBA_LANES:
                ns_rows.append(jnp.zeros((MOBA_LANES - n_blk, tq), F32))
        ns_rows.append(jnp.zeros((HEAD_DIM - REP * MOBA_LANES, tq), F32))
        ns = jnp.concatenate(ns_rows, axis=0).T
        qa, qo = [], []
        for r in range(REP):
            mine = (lane >= HEAD_DIM + r * MOBA_LANES) & (lane < HEAD_DIM + (r + 1) * MOBA_LANES)
            qh = head(g * REP + r).astype(BF16)
            qo.append(qh)
            qa.append(qh + jnp.where(mine, ns, 0.0).astype(BF16))
        qas.append(jnp.concatenate(qa, axis=0))
        qos.append(jnp.concatenate(qo, axis=0))

    def scores(k0, width, qs):
        return [_dot_nt(qs[g], k_ref[pl.ds(k0, width), g * LANES:(g + 1) * LANES]) for g in range(N_KV)]

    pair = 2 * MOBA_BLOCK

    def body(kt, carry):
        k0 = pl.multiple_of(kt * pair, pair)
        ss = scores(k0, pair, qas)
        return tuple(_flash_step(carry[g], ss[g], v_ref[pl.ds(k0, pair), g * LANES:(g + 1) * LANES], g)
                     for g in range(N_KV))

    carry = lax.fori_loop(0, i // 2, body, tuple(_flash_init(REP * tq, g) for g in range(N_KV)))
    odd = (i % 2) == 1
    kp = pl.multiple_of(jnp.maximum(i - 1, 0) * MOBA_BLOCK, MOBA_BLOCK)
    ko = pl.multiple_of(i * MOBA_BLOCK, MOBA_BLOCK)
    s_prev = scores(kp, MOBA_BLOCK, qas)
    s_own = scores(ko, MOBA_BLOCK, qos)
    outs = []
    for g in range(N_KV):
        v = jnp.concatenate([v_ref[pl.ds(kp, MOBA_BLOCK), g * LANES:(g + 1) * LANES],
                             v_ref[pl.ds(ko, MOBA_BLOCK), g * LANES:(g + 1) * LANES]], axis=0)
        s = jnp.concatenate([jnp.where(odd, s_prev[g], NEG), jnp.where(cc <= rr, s_own[g], NEG)], axis=1)
        o = _flash_out(_flash_step(carry[g], s, v, g), g)
        outs += [o[r * tq:(r + 1) * tq] for r in range(REP)]
    o_ref[...] = _compact_heads(outs).astype(BF16)


def _moba_prompt(qm, rows, mobak, mobav, nb, seq):
    tq = MOBA_BLOCK
    nq = seq // tq
    n = qm.shape[0]
    full = lambda wd: pl.BlockSpec((seq, wd), lambda b_, i: (b_, 0))
    return pl.pallas_call(
        functools.partial(_moba_kernel, tq=tq, n_blk=nq),
        grid=(nb, nq),
        in_specs=[pl.BlockSpec((tq, QPAD_W), lambda b_, i: (b_ * nq + i, 0)),
                  full(KV_W), full(2 * LANES), full(2 * LANES)],
        out_specs=pl.BlockSpec((tq, N_HEADS * HEAD_DIM), lambda b_, i: (b_ * nq + i, 0)),
        out_shape=jax.ShapeDtypeStruct((n, N_HEADS * HEAD_DIM), BF16),
        scratch_shapes=[pltpu.VMEM((nq, LANES), F32)],
        compiler_params=_params(2),
        name="moba_prompt",
    )(qm, rows, mobak, mobav)


FFN_CHUNK = 256
FFN_TM = 1024


def _ffn_kernel(x_ref, conv_ref, nsa_ref, moba_ref, wo_ref, g_ref, wgu_ref, wd_ref, o_ref):
    mix = jnp.concatenate([conv_ref[...], nsa_ref[...], moba_ref[...]], axis=1)
    x1 = x_ref[...] + _dot(mix, wo_ref[...])
    h2 = _rms(x1, g_ref[...]).astype(BF16)
    acc = x1
    for c0 in range(0, FFN_HIDDEN, FFN_CHUNK):
        gt = _dot(h2, wgu_ref[:, c0:c0 + FFN_CHUNK])
        up = _dot(h2, wgu_ref[:, FFN_HIDDEN + c0:FFN_HIDDEN + c0 + FFN_CHUNK])
        acc = acc + _dot((gt * jax.nn.sigmoid(gt) * up).astype(BF16), wd_ref[c0:c0 + FFN_CHUNK, :])
    o_ref[...] = acc


def _ffn(x, conv_o, nsa_o, moba_o, wo, g, wgu, wd, tm):
    n = x.shape[0]
    row = lambda wd_: pl.BlockSpec((tm, wd_), lambda i: (i, 0))
    return pl.pallas_call(
        _ffn_kernel,
        grid=(n // tm,),
        in_specs=[row(D_MODEL), row(CONV_CH), row(N_HEADS * HEAD_DIM), row(N_HEADS * HEAD_DIM),
                  _const_spec((D_MODEL, D_MODEL)), _const_spec((1, D_MODEL)),
                  _const_spec((D_MODEL, 2 * FFN_HIDDEN)), _const_spec((FFN_HIDDEN, D_MODEL))],
        out_specs=row(D_MODEL),
        out_shape=jax.ShapeDtypeStruct((n, D_MODEL), F32),
        compiler_params=_params(1),
        name="mix_ffn",
    )(x, conv_o, nsa_o, moba_o, wo, g, wgu, wd)


def _norm_kernel(x_ref, g_ref, o_ref):
    o_ref[...] = _rms(x_ref[...], g_ref[...])


def _final_norm(x, g, tm):
    n = x.shape[0]
    return pl.pallas_call(
        _norm_kernel,
        grid=(n // tm,),
        in_specs=[pl.BlockSpec((tm, D_MODEL), lambda i: (i, 0)), _const_spec((1, D_MODEL))],
        out_specs=pl.BlockSpec((tm, D_MODEL), lambda i: (i, 0)),
        out_shape=jax.ShapeDtypeStruct((n, D_MODEL), F32),
        compiler_params=_params(1),
        name="final_norm",
    )(x, g)


def _q8(q, g):
    row = _iota((8, LANES), 0)
    out = jnp.zeros((8, LANES), F32)
    for r in range(REP):
        hh = g * REP + r
        out = jnp.where(row == r, q[:, hh * LANES:(hh + 1) * LANES], out)
    return pltpu.roll(out, HEAD_DIM, 1) if g == 1 else out


def _gather_pages(page_src, page_dst, sem, n_pages, start):
    def body(p, _):
        cp = pltpu.make_async_copy(page_src(p), page_dst(p), sem)
        if start:
            cp.start()
        else:
            cp.wait()
        return 0
    lax.fori_loop(0, n_pages, body, 0)


def _paged_prologue(src_of, dst_of, sem, n_pages):
    b = pl.program_id(0)
    slot = b % 2

    def run(bb, sl, start):
        _gather_pages(lambda p: src_of(bb, p), lambda p: dst_of(sl, p), sem.at[sl], n_pages, start)

    @pl.when(b == 0)
    def _():
        run(0, 0, True)

    run(b, slot, False)

    @pl.when(b + 1 < pl.num_programs(0))
    def _():
        run(b + 1, 1 - slot, True)

    return slot


def _rank_lanes(score):
    n = score.shape[1]
    s_row = jnp.broadcast_to(score, (n, n))
    s_col = s_row.T
    ii = _iota((n, n), 0)
    jj = _iota((n, n), 1)
    beats = (s_col > s_row) | ((s_col == s_row) & (ii < jj))
    return jnp.sum(jnp.where(beats, 1, 0), axis=0, keepdims=True)


def _cmp_sample_kernel(pt_ref, cache_ref, new_ref, q_ref, wbig_ref, w2_ref, pe_ref, mw_ref, perm_ref,
                       o_ref, idx_ref, x_ref, xb_ref, ab_ref, sem, *, layer, n_pages, past):
    slot = _paged_prologue(lambda b, p: cache_ref.at[layer, pt_ref[b, p]], lambda sl, p: x_ref.at[sl, p], sem, n_pages)
    per_page = PAGE // CMP_STRIDE

    def pair_to_chunks(pp):
        r0 = pl.multiple_of(pp * 2 * per_page, 2 * per_page)
        tr = []
        for t in range(2):
            xp = _dot(x_ref[slot, 2 * pp + t].astype(BF16), perm_ref[...])
            tr.append([xp[hv * LANES:(hv + 1) * LANES, :].T for hv in range(2)])
        for i in range(CMP_STRIDE):
            for hv in range(2):
                piece = jnp.concatenate([tr[t][hv][i * per_page:(i + 1) * per_page, :] for t in range(2)], axis=0)
                xb_ref[hv, pl.ds(r0, 2 * per_page), i * LANES:(i + 1) * LANES] = piece.astype(BF16)

    def to_chunks(t, _):
        for u in range(CMP_PAIRS_PER_TRIP):
            pair_to_chunks(t * CMP_PAIRS_PER_TRIP + u)
        return 0

    lax.fori_loop(0, n_pages // (2 * CMP_PAIRS_PER_TRIP), to_chunks, 0)
    n_tok = n_pages * per_page
    for hv in range(2):
        new_chunk = jnp.concatenate([new_ref[0][:, hv * LANES:(hv + 1) * LANES],
                                     jnp.zeros((1, HALF_CHUNK_W - LANES), F32)], axis=1)
        xb_ref[hv, pl.ds(n_tok, CMP_TAIL), :] = jnp.where(_iota((CMP_TAIL, 1), 0) == 0, new_chunk, 0.0).astype(BF16)
    comp = _compress(xb_ref, n_tok, wbig_ref, w2_ref, pe_ref, ab_ref)
    kc = comp[:, :LANES]
    vc = comp[:, LANES:].astype(BF16)
    q = q_ref[0]
    cmask = (_iota((1, n_tok), 1) * CMP_STRIDE + (2 * CMP_STRIDE - 1)) <= past
    row8 = _iota((8, 1), 0)
    n_sb = past // SEL_BLOCK + 1
    cur = past // SEL_BLOCK
    n_l = mw_ref.shape[1]
    jl = _iota((1, n_l), 1)
    for g in range(N_KV):
        p = _softmax_rows(_dot_nt(_q8(q, g), kc, HI), cmask)
        o_ref[0, g] = _dot(p.astype(BF16), vc)
        imp = jnp.sum(jnp.where(row8 < REP, p, 0.0), axis=0, keepdims=True)
        pslc = _dot(jnp.broadcast_to(imp, (8, n_tok)), mw_ref[...], HI)[0:1, :]
        forced = (jl == 0) | (jl == cur) | (jl == cur - 1)
        score = jnp.where(jl >= n_sb, -jnp.inf, jnp.where(forced, BIG, jnp.where(jl <= cur, pslc, NEG)))
        rank = _rank_lanes(score)
        kk = _iota((N_SEL, n_l), 0)
        idx = jnp.sum(jnp.where(rank == kk, _iota((N_SEL, n_l), 1), 0), axis=1, keepdims=True)
        idx_ref[0, g] = jnp.broadcast_to(idx, (N_SEL, LANES))


def _cmp_sample(layer, pt, cache_t, new, q, wbig, w2, pe, mw, past):
    bd, n_pages = pt.shape
    n_tok = n_pages * (PAGE // CMP_STRIDE)
    blk = lambda *s: pl.BlockSpec((1,) + s, lambda b_, pt_: (b_,) + (0,) * len(s))
    cst = lambda shape: pl.BlockSpec(shape, lambda b_, pt_: (0,) * len(shape), pipeline_mode=pl.Buffered(1))
    gs = pltpu.PrefetchScalarGridSpec(
        num_scalar_prefetch=1, grid=(bd,),
        in_specs=[pl.BlockSpec(memory_space=pl.ANY), blk(1, KV_W), blk(1, QPAD_W),
                  cst((2, HALF_CHUNK_W, CMP_AB_W)), cst((4 * CMP_HIDDEN, KV_W)), cst((2, 8, HALF_CHUNK_W)),
                  cst(mw.shape), cst((PAGE, PAGE))],
        out_specs=[blk(N_KV, 8, LANES), blk(N_KV, N_SEL, LANES)],
        scratch_shapes=[pltpu.VMEM((2, n_pages, KV_W, PAGE), F32),
                        pltpu.VMEM((2, n_tok + CMP_TAIL, HALF_CHUNK_W), BF16),
                        pltpu.VMEM((2, n_tok + CMP_TAIL, CMP_AB_W), F32), pltpu.SemaphoreType.DMA((2,))])
    return pl.pallas_call(
        functools.partial(_cmp_sample_kernel, layer=layer, n_pages=n_pages, past=past),
        grid_spec=gs,
        out_shape=[jax.ShapeDtypeStruct((bd, N_KV, 8, LANES), F32), jax.ShapeDtypeStruct((bd, N_KV, N_SEL, LANES), I32)],
        compiler_params=_params(1, 60 * 1024 * 1024),
        name="cmp_sample",
    )(pt, cache_t, new, q, wbig, w2, pe, mw, jnp.asarray(_chunk_perm(), BF16))


def _chunk_perm():
    per_page = PAGE // CMP_STRIDE
    src = np.arange(PAGE)
    dst = (src % CMP_STRIDE) * per_page + src // CMP_STRIDE
    m = np.zeros((PAGE, PAGE), np.float32)
    m[src, dst] = 1.0
    return m


def _one_query_attention(q8, tiles, k_new, v_new):
    qb = q8.astype(BF16)
    ss = []
    for k_t, _, mask in tiles:
        s = _dot(qb, k_t)
        ss.append(s if mask is None else jnp.where(mask, s, NEG))
    s_new = jnp.sum(q8 * k_new, axis=-1, keepdims=True)
    m = functools.reduce(jnp.maximum, [jnp.max(s, axis=-1, keepdims=True) for s in ss] + [s_new])
    es = [jnp.exp(s - m) for s in ss]
    e_new = jnp.exp(s_new - m)
    l = functools.reduce(jnp.add, [jnp.sum(e, axis=-1, keepdims=True) for e in es]) + e_new
    acc = functools.reduce(jnp.add, [_dot_nt(e.astype(BF16), t[1]) for e, t in zip(es, tiles)])
    return (acc + e_new * v_new) / l


def _selwin_sample_kernel(pt_ref, idx_ref, cache_ref, newsel_ref, win_ref, newwin_ref, q_ref,
                          osel_ref, owin_ref, buf_ref, sem, *, layer, n_cb):
    b = pl.program_id(0)
    per_page = PAGE // SEL_BLOCK
    slot = b % 2

    def gather(bb, sl, start):
        for g in range(N_KV):
            for k in range(N_SEL):
                blk = jnp.minimum(idx_ref[bb, g * N_SEL + k], n_cb - 1)
                cp = pltpu.make_async_copy(cache_ref.at[layer, pt_ref[bb, blk // per_page]], buf_ref.at[sl, g, k],
                                           sem.at[sl])
                if start:
                    cp.start()
                else:
                    cp.wait()

    @pl.when(b == 0)
    def _():
        gather(0, 0, True)

    @pl.when(b + 1 < pl.num_programs(0))
    def _():
        gather(b + 1, 1 - slot, True)

    q = q_ref[0]
    new_sel = newsel_ref[0]
    new_win = newwin_ref[0]
    win_k = win_ref[0, 0, :LANES, :].astype(BF16)
    win_v = win_ref[0, 0, LANES:, :].astype(BF16)
    for g in range(N_KV):
        owin_ref[0, g] = _one_query_attention(_q8(q, g), [(win_k, win_v, None)], new_win[:, :LANES], new_win[:, LANES:])
    gather(b, slot, False)
    lane_blk = _iota((1, PAGE), 1) // SEL_BLOCK
    for g in range(N_KV):
        tiles = []
        for k in range(N_SEL):
            blk = idx_ref[b, g * N_SEL + k]
            mask = (lane_blk == blk % per_page) & (blk < n_cb)
            tiles.append((buf_ref[slot, g, k, :LANES, :].astype(BF16), buf_ref[slot, g, k, LANES:, :].astype(BF16),
                          mask))
        osel_ref[0, g] = _one_query_attention(_q8(q, g), tiles, new_sel[:, :LANES], new_sel[:, LANES:])


def _selwin_sample(layer, pt, idx, cache_t, newsel, win_t, newwin, q, past):
    bd = pt.shape[0]
    wb = win_t.shape[-1]
    blk = lambda *s: pl.BlockSpec((1,) + s, lambda b_, *_: (b_,) + (0,) * len(s))
    gs = pltpu.PrefetchScalarGridSpec(
        num_scalar_prefetch=2, grid=(bd,),
        in_specs=[pl.BlockSpec(memory_space=pl.ANY), blk(1, KV_W),
                  pl.BlockSpec((1, 1, KV_W, wb), lambda b_, *_: (layer, b_, 0, 0)), blk(1, KV_W), blk(1, QPAD_W)],
        out_specs=[blk(N_KV, 8, LANES), blk(N_KV, 8, LANES)],
        scratch_shapes=[pltpu.VMEM((2, N_KV, N_SEL, KV_W, PAGE), F32), pltpu.SemaphoreType.DMA((2,))])
    return pl.pallas_call(
        functools.partial(_selwin_sample_kernel, layer=layer, n_cb=past // SEL_BLOCK),
        grid_spec=gs,
        out_shape=[jax.ShapeDtypeStruct((bd, N_KV, 8, LANES), F32)] * 2,
        compiler_params=_params(1),
        name="selwin_sample",
    )(pt, idx, cache_t, newsel, win_t, newwin, q)


MOBA_CHUNK_PAGES = 8


def _moba_sample_kernel(pt_ref, cache_ref, new_ref, q_ref, o_ref, x_ref, s_ref, sem, *, layer, n_pages, n_blk):
    slot = _paged_prologue(lambda b, p: cache_ref.at[layer, pt_ref[b, p]], lambda sl, p: x_ref.at[sl, p], sem, n_pages)
    q = q_ref[0]
    new = new_ref[0]
    per_blk = MOBA_BLOCK // PAGE
    lane_sq = _iota((LANES, LANES), 1)
    km_t = jnp.zeros((LANES, LANES), F32)
    for j in range(n_blk):
        blk = functools.reduce(jnp.add, [x_ref[slot, j * per_blk + t, :LANES, :] for t in range(per_blk)])
        km_t = jnp.where(lane_sq == j, jnp.sum(blk, axis=1, keepdims=True) * (1.0 / MOBA_BLOCK), km_t)
    jl = _iota((1, LANES), 1)
    row8 = _iota((8, LANES), 0)
    q8s, notsels = [], []
    for g in range(N_KV):
        q8 = _q8(q, g)
        gate = _dot(q8, km_t, HI)
        notsel = jnp.ones((8, LANES), F32)
        for r in range(REP):
            score = jnp.where(jl < n_blk, gate[r:r + 1, :], -jnp.inf)
            sel = (_rank_lanes(score) < MOBA_TOPK) & (jl < n_blk)
            notsel = jnp.where((row8 == r) & sel, 0.0, notsel)
        q8s.append(q8)
        notsels.append(notsel)
    cw = MOBA_CHUNK_PAGES * PAGE
    blk_per_chunk = cw // MOBA_BLOCK
    blk_of_lane = _iota((1, cw), 1) // MOBA_BLOCK
    n_chunks = n_pages // MOBA_CHUNK_PAGES
    for c in range(n_chunks):
        k_t = jnp.concatenate([x_ref[slot, c * MOBA_CHUNK_PAGES + t, :LANES, :] for t in range(MOBA_CHUNK_PAGES)],
                              axis=1).astype(BF16)
        for g in range(N_KV):
            off = jnp.zeros((8, cw), F32)
            for t in range(blk_per_chunk):
                j = c * blk_per_chunk + t
                off = jnp.where(blk_of_lane == t, notsels[g][:, j:j + 1], off)
            s_ref[g, :, c * cw:(c + 1) * cw] = jnp.where(off > 0.5, NEG, _dot(q8s[g].astype(BF16), k_t))
    ls, e_news = [], []
    for g in range(N_KV):
        s = s_ref[g]
        s_new = jnp.sum(q8s[g] * new[:, :LANES], axis=-1, keepdims=True)
        m = jnp.maximum(jnp.max(s, axis=-1, keepdims=True), s_new)
        e = jnp.exp(s - m)
        e_new = jnp.exp(s_new - m)
        ls.append(jnp.sum(e, axis=-1, keepdims=True) + e_new)
        e_news.append(e_new)
        s_ref[g] = e
    acc = [e_news[g] * new[:, LANES:] for g in range(N_KV)]
    for c in range(n_chunks):
        v_t = jnp.concatenate([x_ref[slot, c * MOBA_CHUNK_PAGES + t, LANES:, :] for t in range(MOBA_CHUNK_PAGES)],
                              axis=1).astype(BF16)
        for g in range(N_KV):
            acc[g] = acc[g] + _dot_nt(s_ref[g, :, c * cw:(c + 1) * cw].astype(BF16), v_t)
    for g in range(N_KV):
        o_ref[0, g] = acc[g] / ls[g]


def _moba_sample(layer, pt, cache_t, new, q, past):
    bd, n_pages = pt.shape
    n_blk = past // MOBA_BLOCK
    blk = lambda *s: pl.BlockSpec((1,) + s, lambda b_, pt_: (b_,) + (0,) * len(s))
    gs = pltpu.PrefetchScalarGridSpec(
        num_scalar_prefetch=1, grid=(bd,),
        in_specs=[pl.BlockSpec(memory_space=pl.ANY), blk(1, KV_W), blk(1, QPAD_W)],
        out_specs=blk(N_KV, 8, LANES),
        scratch_shapes=[pltpu.VMEM((2, n_pages, KV_W, PAGE), F32), pltpu.VMEM((N_KV, 8, past), F32),
                        pltpu.SemaphoreType.DMA((2,))])
    return pl.pallas_call(
        functools.partial(_moba_sample_kernel, layer=layer, n_pages=n_pages, n_blk=n_blk),
        grid_spec=gs,
        out_shape=jax.ShapeDtypeStruct((bd, N_KV, 8, LANES), F32),
        compiler_params=_params(1),
        name="moba_sample",
    )(pt, cache_t, new, q)


def _combine_sample_kernel(u_ref, st_ref, w_ref, b_ref, lg_ref, lb_ref, gates_ref, ocmp_ref, osel_ref, owin_ref,
                           omoba_ref, conv_ref, nsa_ref, moba_ref):
    y = u_ref[...] * w_ref[CONV_WIDTH - 1:CONV_WIDTH, :]
    for j in range(CONV_WIDTH - 1):
        y = y + st_ref[j] * w_ref[j:j + 1, :]
    conv_ref[...] = _conv_post(y, b_ref, lg_ref, lb_ref).astype(BF16)
    gates = gates_ref[...]
    w = N_HEADS * HEAD_DIM
    head_of_lane = _iota((1, w), 1) // HEAD_DIM
    acc = jnp.zeros(ocmp_ref.shape, F32)
    for br, ref in enumerate((ocmp_ref, osel_ref, owin_ref)):
        ge = jnp.zeros(ocmp_ref.shape, F32)
        for hh in range(N_HEADS):
            c = br * N_HEADS + hh
            ge = jnp.where(head_of_lane == hh, gates[:, c:c + 1], ge)
        acc = acc + ge * ref[...]
    nsa_ref[...] = acc.astype(BF16)
    moba_ref[...] = omoba_ref[...].astype(BF16)


def _combine_sample(u, st_t, w, b, lg, lb, gates, o_cmp, o_sel, o_win, o_moba):
    bd = u.shape[0]
    w_h = N_HEADS * HEAD_DIM
    return pl.pallas_call(
        _combine_sample_kernel,
        out_shape=[jax.ShapeDtypeStruct((bd, CONV_CH), BF16), jax.ShapeDtypeStruct((bd, w_h), BF16),
                   jax.ShapeDtypeStruct((bd, w_h), BF16)],
        name="combine_sample",
    )(u, st_t, w, b, lg, lb, gates, o_cmp, o_sel, o_win, o_moba)


def _rope_tables(pos):
    half = HEAD_DIM // 2
    inv = ROPE_THETA ** (-jnp.arange(half, dtype=F32) / half)
    ang = pos.astype(F32)[:, None] * inv[None, :]
    cos, sin = jnp.cos(ang), jnp.sin(ang)
    zero = jnp.zeros_like(sin)
    return (jnp.tile(cos, (1, 4)), jnp.tile(jnp.concatenate([-sin, zero], axis=1), (1, 2)),
            jnp.tile(jnp.concatenate([zero, sin], axis=1), (1, 2)))


def _pad_heads(w):
    d = w.shape[:-1]
    w = w.reshape(*d, N_HEADS, HEAD_DIM)
    return jnp.pad(w, [(0, 0)] * len(d) + [(0, 0), (0, LANES - HEAD_DIM)]).reshape(*d, QPAD_W)


def _relayout_w_in(w_in):
    c_q = 2 * CONV_CH
    c_kv = c_q + N_HEADS * HEAD_DIM
    c_g = c_kv + 6 * N_KV * HEAD_DIM
    c_qm = c_g + 3 * N_HEADS
    c_kvm = c_qm + N_HEADS * HEAD_DIM
    gates = jnp.pad(w_in[..., c_g:c_qm], ((0, 0), (0, 0), (0, LANES - 3 * N_HEADS)))
    return jnp.concatenate([w_in[..., :c_q], _pad_heads(w_in[..., c_q:c_kv]), w_in[..., c_kv:c_g],
                            _pad_heads(w_in[..., c_qm:c_kvm]), w_in[..., c_kvm:], gates], axis=-1).astype(BF16)


def _compress_weights(pe_k, pe_v, wk1, wk2, wv1, wv2):
    depth = wk1.shape[0]
    w1 = jnp.stack([wk1, wv1], axis=1).reshape(depth, 2, 2, CMP_STRIDE, HEAD_DIM, CMP_HIDDEN)
    wkv = jnp.einsum('lkaidh,gt->lkigdath', w1, jnp.eye(N_KV, dtype=F32))
    wkv = wkv.reshape(depth, 2, HALF_CHUNK_W, CMP_AB_W).astype(BF16)
    w2 = jnp.stack([wk2, wk2, wv2, wv2], axis=1)
    w2big = jnp.einsum('lshd,st->lshtd', w2, jnp.eye(4, dtype=F32)).reshape(depth, 4 * CMP_HIDDEN, KV_W).astype(BF16)
    pe = jnp.stack([pe_k, pe_v], axis=1).reshape(depth, 2, 2, CMP_STRIDE, 1, HEAD_DIM)
    pe = jnp.broadcast_to(pe, (depth, 2, 2, CMP_STRIDE, N_KV, HEAD_DIM)).reshape(depth, 2, 2, HALF_CHUNK_W)
    return wkv, w2big, jnp.pad(pe, ((0, 0), (0, 0), (0, 6), (0, 0)))


def _slc_matrix(n_c, n_sb):
    ratio = SEL_BLOCK // CMP_STRIDE
    c = np.arange(n_c)[:, None]
    j = np.arange(n_sb)[None, :]
    m = np.where(c == ratio * j, 1.0, 0.0) + np.where((c > ratio * j) & (c < ratio * (j + 1)), 2.0, 0.0) \
        + np.where(c == ratio * (j + 1), 1.0, 0.0)
    return m.astype(np.float32)


def _heads_from_q8(o):
    return jnp.concatenate([o[:, 0, :REP, :HEAD_DIM], o[:, 1, :REP, HEAD_DIM:]], axis=1).reshape(o.shape[0], -1)


def kernel(x_prompt, x_sample, cache_nsa_cmp, cache_nsa_sel, cache_moba, state_nsa_win, state_conv, page_table,
           g_mix, w_in, conv_w, conv_b, conv_ln_g, conv_ln_b, cmp_pe_k, cmp_pe_v, cmp_wk1, cmp_wk2, cmp_wv1, cmp_wv2,
           w_out, g_ffn, w_gate_up, w_down, g_final):
    nb, seq, _ = x_prompt.shape
    bd, dec_seq, _ = x_sample.shape
    depth = g_mix.shape[0]
    n_pages = page_table.shape[1]
    past = n_pages * PAGE
    assert dec_seq == 1 and seq % 512 == 0 and seq // SEL_BLOCK <= HEAD_DIM and 3 * (seq // MOBA_BLOCK) <= HEAD_DIM
    assert past % MOBA_BLOCK == 0 and past // MOBA_BLOCK <= LANES and bd % 8 == 0 and n_pages % MOBA_CHUNK_PAGES == 0
    assert n_pages % (2 * CMP_PAIRS_PER_TRIP) == 0
    n_p = nb * seq
    tm = 512

    w_in_r = _relayout_w_in(w_in)
    wbig, w2big, pe_rows = _compress_weights(cmp_pe_k, cmp_pe_v, cmp_wk1, cmp_wk2, cmp_wv1, cmp_wv2)
    w_out_b, w_gu_b, w_down_b = w_out.astype(BF16), w_gate_up.astype(BF16), w_down.astype(BF16)
    conv_w_p = jnp.pad(conv_w, ((0, 0), (0, CONV_HALO - CONV_WIDTH), (0, 0)))
    tab_p = _rope_tables(jnp.arange(seq))
    tab_s = _rope_tables(jnp.full((bd,), past))
    n_c = seq // CMP_STRIDE
    mwt_p = jnp.asarray(_slc_matrix(n_c, seq // SEL_BLOCK).T)
    n_sb_s = past // SEL_BLOCK + 1
    mw_s = jnp.asarray(np.pad(_slc_matrix(past // CMP_STRIDE, n_sb_s), ((0, 0), (0, -n_sb_s % LANES))))

    feat_major = lambda a: a.transpose(0, 1, 3, 4, 5, 2).reshape(a.shape[0], a.shape[1], KV_W, a.shape[2])
    cmp_t, sel_t, moba_t, win_t = (feat_major(a) for a in (cache_nsa_cmp, cache_nsa_sel, cache_moba, state_nsa_win))

    xp = x_prompt.reshape(n_p, D_MODEL)
    xs = x_sample.reshape(bd, D_MODEL)
    new_p = [[], [], [], [], []]
    new_s = [[], [], [], [], []]
    r2 = lambda a: a.reshape(1, -1)
    for l in range(depth):
        (u, q, qm, gates, cmp_rows, moba_rows, selk, selv, wink, winv, mobak, mobav,
         cmp_n, sel_n, win_n, moba_n) = _proj(xp, r2(g_mix[l]), w_in_r[l], *tab_p, seq, tm, True)
        conv_o = _conv(u, conv_w_p[l], r2(conv_b[l]), r2(conv_ln_g[l]), r2(conv_ln_b[l]), nb, seq, 512)
        comp = _compress_prompt(cmp_rows, wbig[l], w2big[l], pe_rows[l], nb, seq)
        nsa_o = _nsa_prompt(q, gates, comp, selk, selv, wink, winv, mwt_p, nb, seq)
        moba_o = _moba_prompt(qm, moba_rows, mobak, mobav, nb, seq)
        xp = _ffn(xp, conv_o, nsa_o, moba_o, w_out_b[l], r2(g_ffn[l]), w_gu_b[l], w_down_b[l], FFN_TM)
        kv6 = lambda a, n_: a.reshape(n_, -1, 2, N_KV, HEAD_DIM)
        from_feat = lambda a: a.reshape(nb, 2, N_KV, HEAD_DIM, -1).transpose(0, 4, 1, 2, 3)
        new_p[0].append(from_feat(cmp_n))
        new_p[1].append(from_feat(sel_n))
        new_p[2].append(from_feat(moba_n))
        new_p[3].append(from_feat(win_n[:, :, -min(WINDOW, seq):]))
        new_p[4].append(u.reshape(nb, seq, CONV_CH)[:, -(CONV_WIDTH - 1):])

        (u, q, qm, gates, cmp_rows, sel_rows, win_rows, moba_rows) = _proj(xs, r2(g_mix[l]), w_in_r[l], *tab_s, bd, bd,
                                                                           False)
        b3 = lambda a: a.reshape(bd, 1, -1)
        o_cmp, idx = _cmp_sample(l, page_table, cmp_t, b3(cmp_rows), b3(q), wbig[l], w2big[l], pe_rows[l],
                                 mw_s, past)
        o_sel, o_win = _selwin_sample(l, page_table, idx[:, :, :, 0].reshape(bd, N_KV * N_SEL), sel_t,
                                      b3(sel_rows), win_t, b3(win_rows), b3(q), past)
        o_moba = _moba_sample(l, page_table, moba_t, b3(moba_rows), b3(qm), past)
        conv_o, nsa_o, moba_o = _combine_sample(
            u, state_conv[l].transpose(1, 0, 2), conv_w_p[l], r2(conv_b[l]), r2(conv_ln_g[l]), r2(conv_ln_b[l]), gates,
            _heads_from_q8(o_cmp), _heads_from_q8(o_sel), _heads_from_q8(o_win), _heads_from_q8(o_moba))
        xs = _ffn(xs, conv_o, nsa_o, moba_o, w_out_b[l], r2(g_ffn[l]), w_gu_b[l], w_down_b[l], bd)
        new_s[0].append(kv6(cmp_rows, bd))
        new_s[1].append(kv6(sel_rows, bd))
        new_s[2].append(kv6(moba_rows, bd))
        new_s[3].append(jnp.concatenate([state_nsa_win[l], kv6(win_rows, bd)], axis=1)[:, -state_nsa_win.shape[2]:])
        new_s[4].append(jnp.concatenate([state_conv[l], u[:, None, :]], axis=1)[:, -(CONV_WIDTH - 1):])

    y_prompt = _final_norm(xp, r2(g_final), tm).reshape(nb, seq, D_MODEL)
    y_sample = _final_norm(xs, r2(g_final), bd).reshape(bd, 1, D_MODEL)
    return (y_prompt, y_sample,
            jnp.stack(new_p[0]), jnp.stack(new_s[0]),
            jnp.stack(new_p[1]), jnp.stack(new_s[1]),
            jnp.stack(new_p[2]), jnp.stack(new_s[2]),
            jnp.stack(new_p[3]), jnp.stack(new_s[3]),
            jnp.stack(new_p[4]), jnp.stack(new_s[4]))
```
